```python
import math
import jax, jax.numpy as jnp
from jax import lax
import numpy as np

D_MODEL = 1024
BATCH = 8
SEQ = 2048
DEPTH = 4
DEC_BATCH = 128
DEC_SEQ = 8
PAST_LEN = 16384
PAGE_SIZE = 128

N_MIXERS = 2
N_HGRN_LAYERS = (DEPTH + 1) // 2
N_S5_LAYERS = DEPTH // 2
HGRN_HEAD_DIM = 128
HGRN_HEADS = D_MODEL // HGRN_HEAD_DIM
HGRN_CHUNK = 32
S5_GROUP_CH = 16
S5_GROUPS = D_MODEL // S5_GROUP_CH
S5_STATE = 64
S5_DT_MIN = 1e-3
S5_DT_MAX = 1e-1
D_FF = 2816
MACARON_SCALE = 0.5
EPS = 1e-6
GATE_FLOOR = 1e-30

kernel_name = "hgrn2_s5_macaron_hybrid_step"


def _rmsnorm(x, gain):
    xf = x.astype(jnp.float32)
    y = xf * lax.rsqrt(jnp.mean(xf * xf, axis=-1, keepdims=True) + EPS)
    return (y * gain.astype(jnp.float32)).astype(x.dtype)


def _swiglu(h, w_in, w_out):
    a, b = jnp.split(h @ w_in, 2, axis=-1)
    return (jax.nn.silu(a) * b) @ w_out


def _hgrn_lower_bounds(lb_logits):
    p = jax.nn.softmax(lb_logits.astype(jnp.float32), axis=0)
    return jnp.clip(jnp.cumsum(p, axis=0) - p[0], 0.0, 1.0 - 1e-4)


def _hgrn2_chunk_scan(q, k, v, log_f, s0):
    bsz, T = q.shape[0], q.shape[1]
    C = min(HGRN_CHUNK, T)
    n = -(-T // C)
    pad = n * C - T

    def blocks(t):
        t = jnp.pad(t, ((0, 0), (0, pad), (0, 0), (0, 0)))
        return jnp.moveaxis(t.reshape(bsz, n, C, t.shape[2], t.shape[3]), 1, 0)

    causal = jnp.tril(jnp.ones((C, C), dtype=bool))[None, :, :, None, None]

    def step(S, blk):
        qc, kc, vc, gc = blk
        b = jnp.cumsum(gc, axis=1)
        o_inter = jnp.einsum('bthk,bhkv->bthv', qc * jnp.exp(b), S)
        diff = jnp.where(causal, b[:, :, None] - b[:, None, :], 0.0)
        decay = jnp.where(causal, jnp.exp(diff), 0.0)
        scores = jnp.einsum('bthk,btshk,bshk->bths', qc, decay, kc)
        o_intra = jnp.einsum('bths,bshv->bthv', scores, vc)
        b_end = b[:, -1]
        S_new = jnp.exp(b_end)[..., None] * S + jnp.einsum(
            'bshk,bshv->bhkv', kc * jnp.exp(b_end[:, None] - b), vc)
        return S_new, o_inter + o_intra

    s_T, o = lax.scan(step, s0, (blocks(q), blocks(k), blocks(v), blocks(log_f)))
    o = jnp.moveaxis(o, 0, 1).reshape(bsz, n * C, o.shape[3], o.shape[4])[:, :T]
    return o, s_T


def _hgrn2_mixer(h, s0, lb, w_in, gnorm, w_out):
    bsz, T, _ = h.shape
    proj = (h @ w_in).astype(jnp.float32)
    q, f, i, g = jnp.split(proj, 4, axis=-1)
    lbf = lb.astype(jnp.float32)
    fg = lbf + (1.0 - lbf) * jax.nn.sigmoid(f)
    log_f = jnp.log(jnp.maximum(fg, GATE_FLOOR))
    k = (1.0 - lbf) * jax.nn.sigmoid(-f)

    def heads(t):
        return t.reshape(bsz, T, HGRN_HEADS, HGRN_HEAD_DIM)

    o, s_T = _hgrn2_chunk_scan(heads(q), heads(k), heads(i), heads(log_f), s0.astype(jnp.float32))
    o = o * lax.rsqrt(jnp.mean(o * o, axis=-1, keepdims=True) + EPS) * gnorm.astype(jnp.float32)
    o = o.reshape(bsz, T, D_MODEL) * jax.nn.silu(g)
    return o.astype(h.dtype) @ w_out, s_T


def _s5_mixer(h, s0_re, s0_im, a_re, a_im, log_dt, b_re, b_im, c_re, c_im, d_skip, w_glu):
    f32 = jnp.float32
    bsz, T, _ = h.shape
    u = h.astype(f32)
    lam = lax.complex(a_re.astype(f32), a_im.astype(f32))
    dt = jnp.exp(log_dt.astype(f32))[:, None]
    lam_bar = jnp.exp(lam * dt)
    b_bar = ((lam_bar - 1.0) / lam)[..., None] * lax.complex(b_re.astype(f32), b_im.astype(f32))
    u_g = u.reshape(bsz, T, S5_GROUPS, S5_GROUP_CH).astype(jnp.complex64)
    bu = jnp.einsum('btgc,gpc->btgp', u_g, b_bar)
    s0 = lax.complex(s0_re.astype(f32), s0_im.astype(f32))
    bu = bu.at[:, 0].add(lam_bar * s0)

    def combine(l, r):
        return l[0] * r[0], r[0] * l[1] + r[1]

    _, s = lax.associative_scan(combine, (jnp.broadcast_to(lam_bar, bu.shape), bu), axis=1)
    c = lax.complex(c_re.astype(f32), c_im.astype(f32))
    y = jnp.einsum('btgp,gcp->btgc', s, c).real.reshape(bsz, T, D_MODEL) + d_skip.astype(f32) * u
    z = jax.nn.gelu(y).astype(h.dtype)
    a, gate = jnp.split(z @ w_glu, 2, axis=-1)
    return a * jax.nn.sigmoid(gate), s[:, -1].real, s[:, -1].imag


def _trunk(x, st_hgrn, st_re, st_im,
           ln_ffn1, ffn1_w_in, ffn1_w_out, ln_mix, ln_ffn2, ffn2_w_in, ffn2_w_out,
           hgrn_lb_logits, hgrn_w_in, hgrn_gnorm, hgrn_w_out,
           s5_a_re, s5_a_im, s5_log_dt, s5_b_re, s5_b_im, s5_c_re, s5_c_im, s5_d, s5_w_glu,
           ln_final):
    lbs = _hgrn_lower_bounds(hgrn_lb_logits)
    new_hgrn, new_re, new_im = [], [], []
    for i in range(DEPTH):
        x = x + MACARON_SCALE * _swiglu(_rmsnorm(x, ln_ffn1[i]), ffn1_w_in[i], ffn1_w_out[i])
        h = _rmsnorm(x, ln_mix[i])
        j = i // N_MIXERS
        if i % N_MIXERS == 0:
            out, s_T = _hgrn2_mixer(h, st_hgrn[j], lbs[i], hgrn_w_in[j], hgrn_gnorm[j], hgrn_w_out[j])
            new_hgrn.append(s_T)
        else:
            out, sr, si = _s5_mixer(h, st_re[j], st_im[j], s5_a_re[j], s5_a_im[j], s5_log_dt[j],
                                    s5_b_re[j], s5_b_im[j], s5_c_re[j], s5_c_im[j], s5_d[j], s5_w_glu[j])
            new_re.append(sr)
            new_im.append(si)
        x = x + out
        x = x + MACARON_SCALE * _swiglu(_rmsnorm(x, ln_ffn2[i]), ffn2_w_in[i], ffn2_w_out[i])
    return _rmsnorm(x, ln_final), jnp.stack(new_hgrn), jnp.stack(new_re), jnp.stack(new_im)


def setup_inputs(seed: int = 0) -> dict:
    key = jax.random.key(seed)
    ks = iter(jax.random.split(key, 32))
    f32 = jnp.float32

    def nrm(shape, scale):
        return jax.random.normal(next(ks), shape, f32) * scale

    def gain(shape):
        return 1.0 + nrm(shape, 0.02)

    D, F = D_MODEL, D_FF
    G, GC, P = S5_GROUPS, S5_GROUP_CH, S5_STATE
    H, K = HGRN_HEADS, HGRN_HEAD_DIM
    NA, NB = N_HGRN_LAYERS, N_S5_LAYERS
    return {
        "x_prompt": nrm((BATCH, SEQ, D), 1.0),
        "x_sample": nrm((DEC_BATCH, DEC_SEQ, D), 1.0),
        "state_hgrn": nrm((NA, DEC_BATCH, H, K, K), 0.3),
        "state_s5_re": nrm((NB, DEC_BATCH, G, P), 0.1),
        "state_s5_im": nrm((NB, DEC_BATCH, G, P), 0.1),
        "ln_ffn1": gain((DEPTH, D)),
        "ffn1_w_in": nrm((DEPTH, D, 2 * F), D ** -0.5),
        "ffn1_w_out": nrm((DEPTH, F, D), F ** -0.5),
        "ln_mix": gain((DEPTH, D)),
        "ln_ffn2": gain((DEPTH, D)),
        "ffn2_w_in": nrm((DEPTH, D, 2 * F), D ** -0.5),
        "ffn2_w_out": nrm((DEPTH, F, D), F ** -0.5),
        "hgrn_lb_logits": nrm((DEPTH, D), 0.1),
        "hgrn_w_in": nrm((NA, D, 4 * D), D ** -0.5),
        "hgrn_gnorm": gain((NA, K)),
        "hgrn_w_out": nrm((NA, D, D), D ** -0.5),
        "s5_a_re": -0.5 + nrm((NB, G, P), 0.01),
        "s5_a_im": jnp.pi * jnp.arange(P, dtype=f32) + nrm((NB, G, P), 0.01),
        "s5_log_dt": jax.random.uniform(next(ks), (NB, G), f32, math.log(S5_DT_MIN), math.log(S5_DT_MAX)),
        "s5_b_re": nrm((NB, G, P, GC), (2 * GC) ** -0.5),
        "s5_b_im": nrm((NB, G, P, GC), (2 * GC) ** -0.5),
        "s5_c_re": nrm((NB, G, GC, P), (2 * P) ** -0.5),
        "s5_c_im": nrm((NB, G, GC, P), (2 * P) ** -0.5),
        "s5_d": nrm((NB, D), 1.0),
        "s5_w_glu": nrm((NB, D, 2 * D), D ** -0.5),
        "ln_final": gain((D,)),
    }


def reference(x_prompt, x_sample, state_hgrn, state_s5_re, state_s5_im,
              ln_ffn1, ffn1_w_in, ffn1_w_out, ln_mix, ln_ffn2, ffn2_w_in, ffn2_w_out,
              hgrn_lb_logits, hgrn_w_in, hgrn_gnorm, hgrn_w_out,
              s5_a_re, s5_a_im, s5_log_dt, s5_b_re, s5_b_im, s5_c_re, s5_c_im, s5_d, s5_w_glu,
              ln_final):
    weights = (ln_ffn1, ffn1_w_in, ffn1_w_out, ln_mix, ln_ffn2, ffn2_w_in, ffn2_w_out,
               hgrn_lb_logits, hgrn_w_in, hgrn_gnorm, hgrn_w_out,
               s5_a_re, s5_a_im, s5_log_dt, s5_b_re, s5_b_im, s5_c_re, s5_c_im, s5_d, s5_w_glu,
               ln_final)
    bp = x_prompt.shape[0]
    zero_hgrn = jnp.zeros((N_HGRN_LAYERS, bp, HGRN_HEADS, HGRN_HEAD_DIM, HGRN_HEAD_DIM), jnp.float32)
    zero_s5 = jnp.zeros((N_S5_LAYERS, bp, S5_GROUPS, S5_STATE), jnp.float32)
    y_prompt, hgrn_p, re_p, im_p = _trunk(x_prompt, zero_hgrn, zero_s5, zero_s5, *weights)
    y_sample, hgrn_s, re_s, im_s = _trunk(x_sample, state_hgrn, state_s5_re, state_s5_im, *weights)
    return (y_prompt, y_sample, hgrn_p, re_p, im_p, hgrn_s, re_s, im_s)
```

```python
import functools

import jax
import jax.numpy as jnp
from jax import lax
from jax.experimental import pallas as pl
from jax.experimental.pallas import tpu as pltpu

F32 = jnp.float32
BF16 = jnp.bfloat16

EPS = 1e-6
GATE_FLOOR = 1e-30
MACARON_SCALE = 0.5
LB_CLIP = 1.0 - 1e-4

HEAD_DIM = 128
S5_GC = 16
S5_P = 64
S5_L = 16
ROW_BLOCK = 128
VMEM_LIMIT = 52 * 1024 * 1024


def _cparams(sem):
    return pltpu.CompilerParams(dimension_semantics=sem, vmem_limit_bytes=VMEM_LIMIT)


def _rms(x, gain):
    return x * lax.rsqrt(jnp.mean(x * x, axis=-1, keepdims=True) + EPS) * gain


def _sigmoid(x):
    return 1.0 / (1.0 + jnp.exp(-x))


def _ffn_kernel(x_ref, ln_ref, win_ref, wout_ref, g2_ref, *rest, tf, nf, emit_hn, final_norm):
    if emit_hn:
        out_ref, hn_ref, h_scr, acc_scr = rest
    else:
        out_ref, h_scr, acc_scr = rest
    j = pl.program_id(1)

    @pl.when(j == 0)
    def _():
        h_scr[...] = _rms(x_ref[...], ln_ref[...]).astype(BF16)
        acc_scr[...] = jnp.zeros_like(acc_scr)

    ab = jnp.dot(h_scr[...], win_ref[...], preferred_element_type=F32)
    a = ab[:, :tf]
    b = ab[:, tf:]
    g = (a * _sigmoid(a) * b).astype(BF16)
    acc_scr[...] += jnp.dot(g, wout_ref[...], preferred_element_type=F32)

    @pl.when(j == nf - 1)
    def _():
        y = x_ref[...] + MACARON_SCALE * acc_scr[...]
        if final_norm:
            out_ref[...] = _rms(y, g2_ref[...])
        else:
            out_ref[...] = y
        if emit_hn:
            hn_ref[...] = _rms(y, g2_ref[...]).astype(BF16)


def _ffn(x, ln, w_in_r, w_out, gain2, *, emit_hn=False, final_norm=False):
    m, d = x.shape
    f = w_out.shape[0]
    tf = 256
    nf = f // tf
    tm = min(1024, m)
    out_shape = [jax.ShapeDtypeStruct((m, d), F32)]
    out_specs = [pl.BlockSpec((tm, d), lambda i, j: (i, 0))]
    if emit_hn:
        out_shape.append(jax.ShapeDtypeStruct((m, d), BF16))
        out_specs.append(pl.BlockSpec((tm, d), lambda i, j: (i, 0)))
    res = pl.pallas_call(
        functools.partial(_ffn_kernel, tf=tf, nf=nf, emit_hn=emit_hn, final_norm=final_norm),
        grid=(m // tm, nf),
        in_specs=[
            pl.BlockSpec((tm, d), lambda i, j: (i, 0)),
            pl.BlockSpec((1, d), lambda i, j: (0, 0)),
            pl.BlockSpec((d, 2 * tf), lambda i, j: (0, j)),
            pl.BlockSpec((tf, d), lambda i, j: (j, 0)),
            pl.BlockSpec((1, d), lambda i, j: (0, 0)),
        ],
        out_specs=out_specs,
        out_shape=out_shape,
        scratch_shapes=[pltpu.VMEM((tm, d), BF16), pltpu.VMEM((tm, d), F32)],
        compiler_params=_cparams(("parallel", "arbitrary")),
        name="ffn",
    )(x, ln, w_in_r, w_out, gain2)
    return res if emit_hn else res[0]


def _arrange_w_in(w_in, tf=256):
    d, f2 = w_in.shape
    f = f2 // 2
    w = w_in.astype(BF16).reshape(d, 2, f // tf, tf)
    return jnp.transpose(w, (0, 2, 1, 3)).reshape(d, f2)


def _hgrn_gates_kernel(hn_ref, w_ref, lbl_ref, q_ref, k_ref, v_ref, lf_ref, sg_ref, *, layer, d, nh):
    h = hn_ref[...]
    logits = lbl_ref[...]
    e = jnp.exp(logits - jnp.max(logits, axis=0, keepdims=True))
    p = e / jnp.sum(e, axis=0, keepdims=True)
    cum = p[0:1]
    for r in range(1, layer + 1):
        cum = cum + p[r:r + 1]
    lb = jnp.clip(cum - p[0:1], 0.0, LB_CLIP)

    def heads(ref, val):
        for hh in range(nh):
            ref[hh] = val[:, hh * HEAD_DIM:(hh + 1) * HEAD_DIM]

    pq = jnp.dot(h, w_ref[:, 0:d], preferred_element_type=F32)
    heads(q_ref, pq)
    pf = jnp.dot(h, w_ref[:, d:2 * d], preferred_element_type=F32)
    ez = jnp.exp(-jnp.abs(pf))
    r = 1.0 / (1.0 + ez)
    pos = pf >= 0.0
    sig_p = jnp.where(pos, r, ez * r)
    sig_n = jnp.where(pos, ez * r, r)
    fg = lb + (1.0 - lb) * sig_p
    heads(lf_ref, jnp.log(jnp.maximum(fg, GATE_FLOOR)))
    heads(k_ref, (1.0 - lb) * sig_n)
    pv = jnp.dot(h, w_ref[:, 2 * d:3 * d], preferred_element_type=F32)
    heads(v_ref, pv)
    pg = jnp.dot(h, w_ref[:, 3 * d:4 * d], preferred_element_type=F32)
    heads(sg_ref, pg * _sigmoid(pg))


def _hgrn_gates(hn, w_in, lb_logits, layer):
    m, d = hn.shape
    nh = d // HEAD_DIM
    tm = min(512, m)
    hm = jax.ShapeDtypeStruct((nh, m, HEAD_DIM), F32)
    hspec = pl.BlockSpec((nh, tm, HEAD_DIM), lambda i: (0, i, 0))
    return pl.pallas_call(
        functools.partial(_hgrn_gates_kernel, layer=layer, d=d, nh=nh),
        grid=(m // tm,),
        in_specs=[
            pl.BlockSpec((tm, d), lambda i: (i, 0)),
            pl.BlockSpec((d, 4 * d), lambda i: (0, 0)),
            pl.BlockSpec(lb_logits.shape, lambda i: (0, 0)),
        ],
        out_specs=[hspec] * 5,
        out_shape=[hm] * 5,
        compiler_params=_cparams(("parallel",)),
        name="hgrn_gates",
    )(hn, w_in, lb_logits)


def _hgrn_scan_kernel(*refs, tc, nseq, nrb, nt, has_state):
    if has_state:
        q_ref, k_ref, v_ref, lf_ref, sg_ref, gn_ref, s0_ref, o_ref, st_ref, s_scr, lv_scr = refs
    else:
        q_ref, k_ref, v_ref, lf_ref, sg_ref, gn_ref, o_ref, st_ref, s_scr, lv_scr = refs
    R = ROW_BLOCK
    t_idx = pl.program_id(2)

    ti = lax.broadcasted_iota(jnp.int32, (R, R), 0)
    si = lax.broadcasted_iota(jnp.int32, (R, R), 1)
    xr = jnp.bitwise_xor(ti, si)
    lv = jnp.zeros((R, R), jnp.int32)
    kbit = 1
    while kbit < R:
        lv = lv + (xr >= kbit).astype(jnp.int32)
        kbit *= 2
    lv_scr[...] = jnp.where(ti >= si, lv, -1)

    if nseq == 1:
        @pl.when(t_idx == 0)
        def _():
            if has_state:
                s_scr[...] = s0_ref[...].reshape(s_scr.shape)
            else:
                s_scr[...] = jnp.zeros_like(s_scr)

    row = lax.broadcasted_iota(jnp.int32, (R, 1), 0)
    pos_tc = jnp.bitwise_and(row, tc - 1)
    seq_of_row = row // tc
    gn = gn_ref[...]

    def block(rb, carry):
        sl = pl.ds(pl.multiple_of(rb * R, R), R)
        q = q_ref[sl, :]
        kk = k_ref[sl, :]
        v = v_ref[sl, :]
        lf = lf_ref[sl, :]
        if nseq == 1:
            s_in = s_scr[...]
        else:
            s_in = s0_ref[pl.ds(rb * nseq, nseq)].reshape(nseq * HEAD_DIM, HEAD_DIM)
        lvl = lv_scr[...]
        vb = v.astype(BF16)

        b = lf
        sh = 1
        while sh < tc:
            b = b + jnp.where(pos_tc >= sh, pltpu.roll(b, sh, 0), 0.0)
            sh *= 2

        dn = (((1,), (1,)), ((), ()))
        scores = jnp.where(lvl == 0,
                           lax.dot_general(q.astype(BF16), kk.astype(BF16), dn, preferred_element_type=F32), 0.0)
        e_end = b
        m = 1
        level = 1
        while m < tc:
            pos = jnp.bitwise_and(row, 2 * m - 1)
            right = pos >= m
            prev = pltpu.roll(e_end, m, 0)
            w = jnp.exp(jnp.where(right, b - prev, e_end - b))
            s_m = lax.dot_general((q * w).astype(BF16), (kk * w).astype(BF16), dn, preferred_element_type=F32)
            scores = jnp.where(lvl == level, s_m, scores)
            e_end = jnp.where(right, e_end, pltpu.roll(e_end, R - m, 0))
            m *= 2
            level += 1
        o = jnp.dot(scores.astype(BF16), vb, preferred_element_type=F32)

        qe = q * jnp.exp(b)
        ke = kk * jnp.exp(e_end - b)
        if nseq > 1:
            qe = jnp.concatenate([jnp.where(seq_of_row == i, qe, 0.0) for i in range(nseq)], axis=1)
            ke = jnp.concatenate([jnp.where(seq_of_row == i, ke, 0.0) for i in range(nseq)], axis=1)
        o = o + jnp.dot(qe.astype(BF16), s_in.astype(BF16), preferred_element_type=F32)
        ds = lax.dot_general(ke.astype(BF16), vb, (((0,), (0,)), ((), ())), preferred_element_type=F32)
        decs = []
        for i in range(nseq):
            dec_row = jnp.exp(e_end[i * tc:i * tc + 1, :])
            decs.append(jnp.transpose(jnp.broadcast_to(dec_row, (HEAD_DIM, HEAD_DIM))))
        dec = decs[0] if nseq == 1 else jnp.concatenate(decs, axis=0)
        s_new = dec * s_in + ds

        on = o * lax.rsqrt(jnp.mean(o * o, axis=-1, keepdims=True) + EPS) * gn
        o_ref[sl, :] = (on * sg_ref[sl, :]).astype(o_ref.dtype)
        if nseq == 1:
            s_scr[...] = s_new
        else:
            st_ref[pl.ds(rb * nseq, nseq)] = s_new.reshape(nseq, HEAD_DIM, HEAD_DIM)
        return carry

    lax.fori_loop(0, nrb, block, 0)

    if nseq == 1:
        @pl.when(t_idx == nt - 1)
        def _():
            st_ref[...] = s_scr[...].reshape(st_ref.shape)


def _hgrn_scan(q, k, v, lf, sg, gnorm, s0, *, bsz, seq):
    nh, m, _ = q.shape
    R = ROW_BLOCK
    has_state = s0 is not None
    if seq >= R:
        tc, nseq = R, 1
        tt = min(512, seq)
        nt = seq // tt
        nb = 1
        grid = (nh, bsz, nt)
        row_map = lambda h, b, t: (h, b * nt + t, 0)
    else:
        assert has_state
        tc, nseq = seq, R // seq
        tt = min(512, m)
        nt = 1
        nb = tt // seq
        grid = (nh, m // tt, 1)
        row_map = lambda h, b, t: (h, b, 0)
    st_map = lambda h, b, t: (b, h, 0, 0)
    rspec = pl.BlockSpec((None, tt, HEAD_DIM), row_map)
    sspec = pl.BlockSpec((nb, None, HEAD_DIM, HEAD_DIM), st_map)
    in_specs = [rspec] * 5 + [pl.BlockSpec((1, HEAD_DIM), lambda h, b, t: (0, 0))]
    args = [q, k, v, lf, sg, gnorm]
    if has_state:
        in_specs.append(sspec)
        args.append(s0)
    return pl.pallas_call(
        functools.partial(_hgrn_scan_kernel, tc=tc, nseq=nseq, nrb=tt // R, nt=nt, has_state=has_state),
        grid=grid,
        in_specs=in_specs,
        out_specs=[rspec, sspec],
        out_shape=[jax.ShapeDtypeStruct((nh, m, HEAD_DIM), BF16),
                   jax.ShapeDtypeStruct((bsz, nh, HEAD_DIM, HEAD_DIM), F32)],
        scratch_shapes=[pltpu.VMEM((HEAD_DIM, HEAD_DIM), F32), pltpu.VMEM((R, R), jnp.int32)],
        compiler_params=_cparams(("parallel", "parallel", "arbitrary")),
        name="hgrn_scan",
    )(*args)


def _hgrn_out_kernel(x_ref, og_ref, w_ref, out_ref, *, nh):
    og = jnp.concatenate([og_ref[hh] for hh in range(nh)], axis=-1)
    out_ref[...] = x_ref[...] + jnp.dot(og, w_ref[...], preferred_element_type=F32)


def _hgrn_out(x, og, w_out):
    m, d = x.shape
    nh = og.shape[0]
    tm = min(1024, m)
    return pl.pallas_call(
        functools.partial(_hgrn_out_kernel, nh=nh),
        grid=(m // tm,),
        in_specs=[
            pl.BlockSpec((tm, d), lambda i: (i, 0)),
            pl.BlockSpec((nh, tm, HEAD_DIM), lambda i: (0, i, 0)),
            pl.BlockSpec((d, d), lambda i: (0, 0)),
        ],
        out_specs=pl.BlockSpec((tm, d), lambda i: (i, 0)),
        out_shape=jax.ShapeDtypeStruct((m, d), F32),
        compiler_params=_cparams(("parallel",)),
        name="hgrn_out",
    )(x, og, w_out)


def _split3(a):
    hi = a.astype(BF16)
    r1 = a - hi.astype(F32)
    mid = r1.astype(BF16)
    lo = (r1 - mid.astype(F32)).astype(BF16)
    return hi, mid, lo


def _dot_precise(a, b):
    ah, am, al = _split3(a)
    bh, bm, bl = _split3(b)
    d = functools.partial(jnp.dot, preferred_element_type=F32)
    return (d(ah, bh) + (d(ah, bm) + d(am, bh))) + ((d(am, bm) + d(ah, bl)) + d(al, bh))


def _s5_prep_kernel(arc_ref, aic_ref, arr_ref, air_ref, ldt_ref, x1_ref, x2_ref, y1_ref, y2_ref,
                    w1_ref, w2_ref, w3_ref, pw_ref, *, gb):
    L, GC, P = S5_L, S5_GC, S5_P
    W = L * GC
    lane_t = lax.broadcasted_iota(jnp.int32, (1, W), 1) // GC
    row_s = lax.broadcasted_iota(jnp.int32, (W, 1), 0) // GC
    sgn_l = jnp.where(lax.broadcasted_iota(jnp.int32, (1, 2 * P), 1) < P, -1.0, 1.0)
    sgn_s = jnp.where(lax.broadcasted_iota(jnp.int32, (2 * P, 1), 0) < P, 1.0, -1.0)
    lane = lax.broadcasted_iota(jnp.int32, (1, W), 1)
    row16 = lax.broadcasted_iota(jnp.int32, (16, 1), 0)

    def cmul(ar, ai, br, bi):
        return ar * br - ai * bi, ar * bi + ai * br

    for g in range(gb):
        dt = jnp.exp(ldt_ref[g])
        a_re_c, a_im_c = arc_ref[g], aic_ref[g]
        er = jnp.exp(a_re_c * dt)
        lbr_c, lbi_c = er * jnp.cos(a_im_c * dt), er * jnp.sin(a_im_c * dt)
        a_re_r, a_im_r = arr_ref[g], air_ref[g]
        er_r = jnp.exp(a_re_r * dt)
        lbr_r, lbi_r = er_r * jnp.cos(a_im_r * dt), er_r * jnp.sin(a_im_r * dt)
        den = a_re_r * a_re_r + a_im_r * a_im_r
        nr, ni = lbr_r - 1.0, lbi_r
        f_r = (nr * a_re_r + ni * a_im_r) / den
        f_i = (ni * a_re_r - nr * a_im_r) / den

        pc = [(jnp.ones_like(lbr_c), jnp.zeros_like(lbr_c))]
        pr = [(jnp.ones_like(lbr_r), jnp.zeros_like(lbr_r))]
        for _ in range(L):
            pc.append(cmul(pc[-1][0], pc[-1][1], lbr_c, lbi_c))
            pr.append(cmul(pr[-1][0], pr[-1][1], lbr_r, lbi_r))

        q_re = jnp.zeros((2 * P, W), F32)
        q_im = jnp.zeros((2 * P, W), F32)
        for t in range(L):
            sel = lane_t == t
            q_re = jnp.where(sel, pc[t][0], q_re)
            q_im = jnp.where(sel, pc[t][1], q_im)
        y1, y2 = y1_ref[g], y2_ref[g]
        mp = sgn_s * (q_re * y1) - q_im * y2
        q1_re, q1_im = cmul(q_re, q_im, lbr_c, lbi_c)
        w3_ref[g] = (sgn_s * (q1_re * y1) - q1_im * y2).astype(BF16)

        cf_re = jnp.zeros((W, 2 * P), F32)
        cf_im = jnp.zeros((W, 2 * P), F32)
        for s in range(L):
            sel = row_s == s
            cf_re = jnp.where(sel, pr[L - 1 - s][0], cf_re)
            cf_im = jnp.where(sel, pr[L - 1 - s][1], cf_im)
        cf_re, cf_im = cmul(cf_re, cf_im, f_r, f_i)
        x1, x2 = x1_ref[g], x2_ref[g]
        w2_ref[g] = (cf_re * x1 + (cf_im * sgn_l) * x2).astype(BF16)

        bbp = f_r * x1[0:GC] + (f_i * sgn_l) * x2[0:GC]
        base = _dot_precise(bbp, mp)
        blocks = [base]
        for s in range(1, L):
            blocks.append(jnp.where(lane >= s * GC, pltpu.roll(base, s * GC, 1), 0.0))
        w1_ref[g] = jnp.concatenate(blocks, axis=0).astype(BF16)

        tab = jnp.zeros((16, 2 * P), F32)
        cur = pr[L]
        for k in range(8):
            if k == 7:
                cur = pr[L // 2]
            tab = jnp.where(row16 == k, cur[0], tab)
            tab = jnp.where(row16 == 8 + k, cur[1] * sgn_l, tab)
            cur = cmul(cur[0], cur[1], cur[0], cur[1])
        pw_ref[g] = tab


def _s5_prep(a_re, a_im, log_dt, b_re, b_im, c_re, c_im):
    g, p = a_re.shape
    L, GC = S5_L, S5_GC
    W = L * GC
    gb = 8
    a_re2 = jnp.concatenate([a_re, a_re], axis=1)
    a_im2 = jnp.concatenate([a_im, a_im], axis=1)
    arc, aic = a_re2[:, :, None], a_im2[:, :, None]
    arr, air = a_re2[:, None, :], a_im2[:, None, :]
    ldt = log_dt[:, None, None]
    btr, bti = jnp.transpose(b_re, (0, 2, 1)), jnp.transpose(b_im, (0, 2, 1))
    x1 = jnp.tile(jnp.concatenate([btr, bti], axis=2), (1, L, 1))
    x2 = jnp.tile(jnp.concatenate([bti, btr], axis=2), (1, L, 1))
    ctr, cti = jnp.transpose(c_re, (0, 2, 1)), jnp.transpose(c_im, (0, 2, 1))
    y1 = jnp.tile(jnp.concatenate([ctr, cti], axis=1), (1, 1, L))
    y2 = jnp.tile(jnp.concatenate([cti, ctr], axis=1), (1, 1, L))

    def spec(shape):
        return pl.BlockSpec((gb,) + shape, lambda i: (i, 0, 0))

    return pl.pallas_call(
        functools.partial(_s5_prep_kernel, gb=gb),
        grid=(g // gb,),
        in_specs=[spec((2 * p, 1)), spec((2 * p, 1)), spec((1, 2 * p)), spec((1, 2 * p)), spec((1, 1)),
                  spec((W, 2 * p)), spec((W, 2 * p)), spec((2 * p, W)), spec((2 * p, W))],
        out_specs=[spec((W, W)), spec((W, 2 * p)), spec((2 * p, W)), spec((16, 2 * p))],
        out_shape=[jax.ShapeDtypeStruct((g, W, W), BF16), jax.ShapeDtypeStruct((g, W, 2 * p), BF16),
                   jax.ShapeDtypeStruct((g, 2 * p, W), BF16), jax.ShapeDtypeStruct((g, 16, 2 * p), F32)],
        compiler_params=_cparams(("parallel",)),
        name="s5_prep",
    )(arc, aic, arr, air, ldt, x1, x2, y1, y2)


def _s5_core_kernel(*refs, gb, nj, bsz, pw_row, has_state):
    if has_state:
        u_ref, w1_ref, w2_ref, w3_ref, pw_ref, s0_ref, y_ref, fin_ref, x_scr = refs
    else:
        u_ref, w1_ref, w2_ref, w3_ref, pw_ref, y_ref, fin_ref, x_scr = refs
    nc = bsz * nj
    dot = functools.partial(jnp.dot, preferred_element_type=F32)
    jrow = jnp.bitwise_and(lax.broadcasted_iota(jnp.int32, (nc, 1), 0), nj - 1)

    def cscale(s, wa, wb):
        return s * wa + pltpu.roll(s, S5_P, 1) * wb

    for g in range(gb):
        u = u_ref[g]
        x = dot(u, w2_ref[g])
        pw = pw_ref[g]
        if nj > 1:
            sh, k = 1, 0
            while sh < nj:
                shifted = jnp.where(jrow >= sh, pltpu.roll(x, sh, 0), 0.0)
                x = x + cscale(shifted, pw[k:k + 1], pw[8 + k:9 + k])
                sh *= 2
                k += 1
            s_in = jnp.where(jrow >= 1, pltpu.roll(x, 1, 0), 0.0)
            x_scr[...] = x
            fin_ref[g] = x_scr[pl.ds(nj - 1, bsz, stride=nj), :]
        else:
            s_in = s0_ref[g]
            fin_ref[g] = x + cscale(s_in, pw[pw_row:pw_row + 1], pw[8 + pw_row:9 + pw_row])
        y = dot(u, w1_ref[g]) + dot(s_in.astype(BF16), w3_ref[g])
        y_ref[g] = y.astype(y_ref.dtype)


def _s5_core(u, w1, w2, w3, pw, s0, *, bsz, nj, L):
    g, nc, w = u.shape
    p2 = 2 * S5_P
    gb = 8
    has_state = s0 is not None
    assert has_state == (nj == 1)
    full = L == S5_L
    assert full or 2 * L == S5_L
    g3 = lambda i: (i, 0, 0)
    in_specs = [
        pl.BlockSpec((gb, nc, w), g3),
        pl.BlockSpec((gb, w, w), g3),
        pl.BlockSpec((gb, w, p2), g3 if full else (lambda i: (i, 1, 0))),
        pl.BlockSpec((gb, p2, w), g3),
        pl.BlockSpec((gb, 16, p2), g3),
    ]
    args = [u, w1, w2, w3, pw]
    if has_state:
        in_specs.append(pl.BlockSpec((gb, bsz, p2), g3))
        args.append(s0)
    return pl.pallas_call(
        functools.partial(_s5_core_kernel, gb=gb, nj=nj, bsz=bsz, pw_row=0 if full else 7, has_state=has_state),
        grid=(g // gb,),
        in_specs=in_specs,
        out_specs=[pl.BlockSpec((gb, nc, w), g3), pl.BlockSpec((gb, bsz, p2), g3)],
        out_shape=[jax.ShapeDtypeStruct((g, nc, w), BF16), jax.ShapeDtypeStruct((g, bsz, p2), F32)],
        scratch_shapes=[pltpu.VMEM((nc, p2), F32)],
        compiler_params=_cparams(("parallel",)),
        name="s5_core",
    )(*args)


def _gelu_tanh(x):
    return 0.5 * x * (1.0 + jnp.tanh(0.7978845608028654 * (x + 0.044715 * (x * x * x))))


def _s5_out_kernel(x_ref, ln_ref, y_ref, d_ref, w_ref, out_ref, *, d):
    x = x_ref[...]
    h = _rms(x, ln_ref[...])
    y = y_ref[...].astype(F32) + d_ref[...] * h
    z = _gelu_tanh(y).astype(BF16)
    ag = jnp.dot(z, w_ref[...], preferred_element_type=F32)
    out_ref[...] = x + ag[:, :d] * _sigmoid(ag[:, d:])


def _s5_out(x, ln, y, d_skip, w_glu):
    m, d = x.shape
    tm = min(1024, m)
    return pl.pallas_call(
        functools.partial(_s5_out_kernel, d=d),
        grid=(m // tm,),
        in_specs=[
            pl.BlockSpec((tm, d), lambda i: (i, 0)),
            pl.BlockSpec((1, d), lambda i: (0, 0)),
            pl.BlockSpec((tm, d), lambda i: (i, 0)),
            pl.BlockSpec((1, d), lambda i: (0, 0)),
            pl.BlockSpec((d, 2 * d), lambda i: (0, 0)),
        ],
        out_specs=pl.BlockSpec((tm, d), lambda i: (i, 0)),
        out_shape=jax.ShapeDtypeStruct((m, d), F32),
        compiler_params=_cparams(("parallel",)),
        name="s5_out",
    )(x, ln, y, d_skip, w_glu)


def _trunk(x3, st_hgrn, st_re, st_im, wts):
    bsz, seq, d = x3.shape
    m = bsz * seq
    x = x3.reshape(m, d)
    depth = wts["ln_ffn1"].shape[0]
    g = d // S5_GC
    new_hgrn, new_re, new_im = [], [], []
    for i in range(depth):
        j = i // 2
        x, hn = _ffn(x, wts["ln_ffn1"][i:i + 1], wts["ffn1_w_in"][i], wts["ffn1_w_out"][i],
                     wts["ln_mix"][i:i + 1], emit_hn=True)
        if i % 2 == 0:
            q, k, v, lf, sg = _hgrn_gates(hn, wts["hgrn_w_in"][j], wts["hgrn_lb_logits"], i)
            s0 = None if st_hgrn is None else st_hgrn[j]
            og, s_t = _hgrn_scan(q, k, v, lf, sg, wts["hgrn_gnorm"][j:j + 1], s0, bsz=bsz, seq=seq)
            new_hgrn.append(s_t)
            x = _hgrn_out(x, og, wts["hgrn_w_out"][j])
        else:
            L = S5_L if seq % S5_L == 0 else S5_L // 2
            nj = seq // L
            w1, w2, w3, pw = wts["s5_mats"][j]
            u = hn.reshape(bsz, nj, L, g, S5_GC).transpose(3, 0, 1, 2, 4).reshape(g, bsz * nj, L * S5_GC)
            s0 = None if st_re is None else jnp.transpose(jnp.concatenate([st_re[j], st_im[j]], axis=-1), (1, 0, 2))
            y, fin = _s5_core(u, w1, w2, w3, pw, s0, bsz=bsz, nj=nj, L=L)
            fin = jnp.transpose(fin, (1, 0, 2))
            y = y.reshape(g, bsz, nj, L, S5_GC).transpose(1, 2, 3, 0, 4).reshape(m, d)
            new_re.append(fin[..., :S5_P])
            new_im.append(fin[..., S5_P:])
            x = _s5_out(x, wts["ln_mix"][i:i + 1], y, wts["s5_d"][j:j + 1], wts["s5_w_glu"][j])
        last = i == depth - 1
        gain2 = wts["ln_final"] if last else wts["ln_ffn2"][i:i + 1]
        x = _ffn(x, wts["ln_ffn2"][i:i + 1], wts["ffn2_w_in"][i], wts["ffn2_w_out"][i], gain2, final_norm=last)
    return x.reshape(bsz, seq, d), jnp.stack(new_hgrn), jnp.stack(new_re), jnp.stack(new_im)


def kernel(x_prompt, x_sample, state_hgrn, state_s5_re, state_s5_im, ln_ffn1, ffn1_w_in, ffn1_w_out, ln_mix, ln_ffn2, ffn2_w_in, ffn2_w_out, hgrn_lb_logits, hgrn_w_in, hgrn_gnorm, hgrn_w_out, s5_a_re, s5_a_im, s5_log_dt, s5_b_re, s5_b_im, s5_c_re, s5_c_im, s5_d, s5_w_glu, ln_final):
    depth = ln_ffn1.shape[0]
    wts = dict(
        ln_ffn1=ln_ffn1, ln_mix=ln_mix, ln_ffn2=ln_ffn2, ln_final=ln_final.reshape(1, -1),
        ffn1_w_in=[_arrange_w_in(ffn1_w_in[i]) for i in range(depth)],
        ffn1_w_out=ffn1_w_out.astype(BF16),
        ffn2_w_in=[_arrange_w_in(ffn2_w_in[i]) for i in range(depth)],
        ffn2_w_out=ffn2_w_out.astype(BF16),
        hgrn_lb_logits=hgrn_lb_logits, hgrn_w_in=hgrn_w_in.astype(BF16), hgrn_gnorm=hgrn_gnorm,
        hgrn_w_out=hgrn_w_out.astype(BF16), s5_d=s5_d, s5_w_glu=s5_w_glu.astype(BF16),
        s5_mats=[_s5_prep(s5_a_re[j], s5_a_im[j], s5_log_dt[j], s5_b_re[j], s5_b_im[j], s5_c_re[j], s5_c_im[j])
                 for j in range(s5_a_re.shape[0])],
    )
    y_p, hg_p, re_p, im_p = _trunk(x_prompt, None, None, None, wts)
    y_s, hg_s, re_s, im_s = _trunk(x_sample, state_hgrn, state_s5_re, state_s5_im, wts)
    return (y_p, y_s, hg_p, re_p, im_p, hg_s, re_s, im_s)
```

```python
import functools

import numpy as np
import jax
import jax.numpy as jnp
from jax import lax
from jax.experimental import pallas as pl
from jax.experimental.pallas import tpu as pltpu

F32 = jnp.float32
BF16 = jnp.bfloat16

EPS = 1e-6
GATE_FLOOR = 1e-30
MACARON_SCALE = 0.5
LB_CLIP = 1.0 - 1e-4

HEAD_DIM = 128
S5_GC = 16
S5_P = 64
S5_L = 8
S5_GL = 8
ROW_BLOCK = 128
FFN_TM = 1024
FFN_TF = 256
HGRN_TT = 512
VMEM_LIMIT = 52 * 1024 * 1024


def _cparams(sem):
    return pltpu.CompilerParams(dimension_semantics=sem, vmem_limit_bytes=VMEM_LIMIT)


def _rms(x, gain):
    return x * lax.rsqrt(jnp.mean(x * x, axis=-1, keepdims=True) + EPS) * gain


def _sigmoid(x):
    return 1.0 / (1.0 + jnp.exp(-x))


def _ffn_kernel(x_ref, ln_ref, wa_ref, wb_ref, wout_ref, g2_ref, *rest, nf, hn_dtype, final_norm, cast_w):
    rest = list(rest)
    out_ref = rest.pop(0)
    hn_ref = rest.pop(0) if hn_dtype is not None else None
    if cast_w:
        wa_bf_ref, wb_bf_ref, wout_bf_ref = rest[:3]
        rest = rest[3:]
    h_scr, acc_scr = rest
    j = pl.program_id(1)

    @pl.when(j == 0)
    def _():
        h_scr[...] = _rms(x_ref[...], ln_ref[...]).astype(BF16)
        acc_scr[...] = jnp.zeros_like(acc_scr)

    wa, wb, wo = wa_ref[...], wb_ref[...], wout_ref[...]
    if cast_w:
        wa, wb, wo = wa.astype(BF16), wb.astype(BF16), wo.astype(BF16)
        wa_bf_ref[...] = wa
        wb_bf_ref[...] = wb
        wout_bf_ref[...] = wo
    h = h_scr[...]
    a = jnp.dot(h, wa, preferred_element_type=F32)
    b = jnp.dot(h, wb, preferred_element_type=F32)
    g = (a * _sigmoid(a) * b).astype(BF16)
    acc_scr[...] += jnp.dot(g, wo, preferred_element_type=F32)

    @pl.when(j == nf - 1)
    def _():
        y = x_ref[...] + MACARON_SCALE * acc_scr[...]
        if final_norm:
            out_ref[...] = _rms(y, g2_ref[...])
        else:
            out_ref[...] = y
        if hn_ref is not None:
            hn_ref[...] = _rms(y, g2_ref[...]).astype(hn_ref.dtype)


def _ffn(x, ln, w_in, w_out, layer, gain2, *, hn_dtype=None, final_norm=False):
    m, d = x.shape
    f = w_out.shape[1]
    tf = FFN_TF
    nf = f // tf
    tm = min(FFN_TM, m)
    cast_w = not isinstance(w_in, tuple)
    if cast_w:
        assert m == tm
        wa, wb, boff = w_in, w_in, nf
    else:
        (wa, wb), boff = w_in, 0
    row = lambda i, j: (i, 0)
    out_shape = [jax.ShapeDtypeStruct((m, d), F32)]
    out_specs = [pl.BlockSpec((tm, d), row)]
    if hn_dtype is not None:
        out_shape.append(jax.ShapeDtypeStruct((m, d), hn_dtype))
        out_specs.append(pl.BlockSpec((tm, d), row))
    if cast_w:
        out_shape += [jax.ShapeDtypeStruct((1, d, f), BF16)] * 2 + [jax.ShapeDtypeStruct((1, f, d), BF16)]
        out_specs += [pl.BlockSpec((None, d, tf), lambda i, j: (0, 0, j))] * 2
        out_specs += [pl.BlockSpec((None, tf, d), lambda i, j: (0, j, 0))]
    res = pl.pallas_call(
        functools.partial(_ffn_kernel, nf=nf, hn_dtype=hn_dtype, final_norm=final_norm, cast_w=cast_w),
        grid=(m // tm, nf),
        in_specs=[
            pl.BlockSpec((tm, d), row),
            pl.BlockSpec((1, d), lambda i, j: (0, 0)),
            pl.BlockSpec((None, d, tf), lambda i, j: (layer, 0, j)),
            pl.BlockSpec((None, d, tf), lambda i, j: (layer, 0, boff + j)),
            pl.BlockSpec((None, tf, d), lambda i, j: (layer, j, 0)),
            pl.BlockSpec((1, d), lambda i, j: (0, 0)),
        ],
        out_specs=out_specs,
        out_shape=out_shape,
        scratch_shapes=[pltpu.VMEM((tm, d), BF16), pltpu.VMEM((tm, d), F32)],
        compiler_params=_cparams(("parallel", "arbitrary")),
        name="ffn_cast" if cast_w else "ffn",
    )(x, ln, wa, wb, w_out, gain2)
    res = list(res)
    out = res.pop(0)
    hn = res.pop(0) if hn_dtype is not None else None
    w_bf = ((res[0], res[1]), res[2]) if cast_w else None
    return out, hn, w_bf


def _hgrn_gates_kernel(hn_ref, w_ref, lbl_ref, cum_ref, q_ref, k_ref, v_ref, b_ref, sg_ref, *, layer, d, nh):
    h = hn_ref[...]
    logits = lbl_ref[...]
    e = jnp.exp(logits - jnp.max(logits, axis=0, keepdims=True))
    p = e / jnp.sum(e, axis=0, keepdims=True)
    cum = p[0:1]
    for r in range(1, layer + 1):
        cum = cum + p[r:r + 1]
    lb = jnp.clip(cum - p[0:1], 0.0, LB_CLIP)

    def heads(ref, val):
        for hh in range(nh):
            ref[hh] = val[:, hh * HEAD_DIM:(hh + 1) * HEAD_DIM].astype(ref.dtype)

    pq = jnp.dot(h, w_ref[:, 0:d], preferred_element_type=F32)
    heads(q_ref, pq)
    pf = jnp.dot(h, w_ref[:, d:2 * d], preferred_element_type=F32)
    ez = jnp.exp(-jnp.abs(pf))
    r = 1.0 / (1.0 + ez)
    pos = pf >= 0.0
    sig_p = jnp.where(pos, r, ez * r)
    sig_n = jnp.where(pos, ez * r, r)
    fg = lb + (1.0 - lb) * sig_p
    hi, mid, lo = _split3(jnp.log(jnp.maximum(fg, GATE_FLOOR)))
    cum = cum_ref[...]
    dot = functools.partial(jnp.dot, preferred_element_type=F32)
    R = ROW_BLOCK
    for rb in range(h.shape[0] // R):
        rows = slice(rb * R, (rb + 1) * R)
        b = (dot(cum, hi[rows]) + dot(cum, mid[rows])) + dot(cum, lo[rows])
        for hh in range(nh):
            b_ref[hh, rows, :] = b[:, hh * HEAD_DIM:(hh + 1) * HEAD_DIM]
    heads(k_ref, (1.0 - lb) * sig_n)
    pv = jnp.dot(h, w_ref[:, 2 * d:3 * d], preferred_element_type=F32)
    heads(v_ref, pv)
    pg = jnp.dot(h, w_ref[:, 3 * d:4 * d], preferred_element_type=F32)
    heads(sg_ref, pg * _sigmoid(pg))


def _hgrn_gates(hn, w_in, layer_j, lb_logits, layer, cum):
    m, d = hn.shape
    nh = d // HEAD_DIM
    tm = min(512, m)
    hm = lambda dt: jax.ShapeDtypeStruct((nh, m, HEAD_DIM), dt)
    hspec = pl.BlockSpec((nh, tm, HEAD_DIM), lambda i: (0, i, 0))
    return pl.pallas_call(
        functools.partial(_hgrn_gates_kernel, layer=layer, d=d, nh=nh),
        grid=(m // tm,),
        in_specs=[
            pl.BlockSpec((tm, d), lambda i: (i, 0)),
            pl.BlockSpec((None, d, 4 * d), lambda i: (layer_j, 0, 0)),
            pl.BlockSpec(lb_logits.shape, lambda i: (0, 0)),
            pl.BlockSpec(cum.shape, lambda i: (0, 0)),
        ],
        out_specs=[hspec] * 5,
        out_shape=[hm(BF16), hm(BF16), hm(BF16), hm(F32), hm(BF16)],
        compiler_params=_cparams(("parallel",)),
        name="hgrn_gates",
    )(hn, w_in, lb_logits, cum)


def _anchor(b, m):
    R, n = b.shape
    if m >= 8:
        parts = [jnp.broadcast_to(b[blk * 2 * m + m - 1:blk * 2 * m + m, :], (2 * m, n)) for blk in range(R // (2 * m))]
        return parts[0] if len(parts) == 1 else jnp.concatenate(parts, axis=0)
    if m == 1:
        odd = jnp.bitwise_and(lax.broadcasted_iota(jnp.int32, (R, 1), 0), 1) == 1
        return jnp.where(odd, pltpu.roll(b, 1, 0), b)
    b3 = b.reshape(R // 8, 8, n)
    if m == 4:
        a3 = jnp.broadcast_to(b3[:, 3:4, :], b3.shape)
    else:
        sub = lax.broadcasted_iota(jnp.int32, (1, 8, 1), 1)
        a3 = jnp.where(sub < 4, jnp.broadcast_to(b3[:, 1:2, :], b3.shape), jnp.broadcast_to(b3[:, 5:6, :], b3.shape))
    return a3.reshape(R, n)


def _block_end(b, tc):
    R, n = b.shape
    if tc == 8:
        b3 = b.reshape(R // 8, 8, n)
        return jnp.broadcast_to(b3[:, 7:8, :], b3.shape).reshape(R, n)
    parts = [jnp.broadcast_to(b[blk * tc + tc - 1:blk * tc + tc, :], (tc, n)) for blk in range(R // tc)]
    return parts[0] if len(parts) == 1 else jnp.concatenate(parts, axis=0)


def _hgrn_scan_kernel(*refs, tc, nseq, nrb, nt, has_state):
    if has_state:
        q_ref, k_ref, v_ref, b_ref, sg_ref, gn_ref, msk_ref, s0_ref, o_ref, st_ref, s_scr = refs
    else:
        q_ref, k_ref, v_ref, b_ref, sg_ref, gn_ref, msk_ref, o_ref, st_ref, s_scr = refs
    R = ROW_BLOCK
    t_idx = pl.program_id(2)
    dot = functools.partial(jnp.dot, preferred_element_type=F32)
    dn = (((1,), (1,)), ((), ()))

    if nseq == 1:
        @pl.when(t_idx == 0)
        def _():
            if has_state:
                s_scr[...] = s0_ref[...].reshape(s_scr.shape)
            else:
                s_scr[...] = jnp.zeros_like(s_scr)

    seq_of_row = lax.broadcasted_iota(jnp.int32, (R, 1), 0) // tc
    gn = gn_ref[...]

    def local_part(rb):
        sl = pl.ds(rb * R, R)
        q = q_ref[sl, :].astype(F32)
        kk = k_ref[sl, :].astype(F32)
        vb = v_ref[sl, :]
        b = b_ref[sl, :]
        scores = lax.dot_general(q_ref[sl, :], k_ref[sl, :], dn, preferred_element_type=F32) * msk_ref[0]
        m, level = 1, 1
        while m < tc:
            w = jnp.exp(-jnp.abs(b - _anchor(b, m)))
            s_m = lax.dot_general((q * w).astype(BF16), (kk * w).astype(BF16), dn, preferred_element_type=F32)
            scores = scores + s_m * msk_ref[level]
            m *= 2
            level += 1
        e_end = _block_end(b, tc)
        qe = q * jnp.exp(b)
        ke = kk * jnp.exp(e_end - b)
        if nseq > 1:
            qe = jnp.concatenate([jnp.where(seq_of_row == i, qe, 0.0) for i in range(nseq)], axis=1)
            ke = jnp.concatenate([jnp.where(seq_of_row == i, ke, 0.0) for i in range(nseq)], axis=1)
        ds = lax.dot_general(ke.astype(BF16), vb, (((0,), (0,)), ((), ())), preferred_element_type=F32)
        decs = []
        for i in range(nseq):
            dec_row = jnp.exp(e_end[i * tc:i * tc + 1, :])
            decs.append(jnp.transpose(jnp.broadcast_to(dec_row, (HEAD_DIM, HEAD_DIM))))
        dec = decs[0] if nseq == 1 else jnp.concatenate(decs, axis=0)
        return jnp.concatenate([scores.astype(BF16), qe.astype(BF16)], axis=1), vb, ds, dec

    parts = [local_part(rb) for rb in range(nrb)]

    s_run = s_scr[...] if nseq == 1 else None
    for rb in range(nrb):
        sq, vb, ds, dec = parts[rb]
        sl = pl.ds(rb * R, R)
        if nseq == 1:
            s_in = s_run
        else:
            s_in = s0_ref[rb * nseq:(rb + 1) * nseq].reshape(nseq * HEAD_DIM, HEAD_DIM)
        o = dot(sq, jnp.concatenate([vb, s_in.astype(BF16)], axis=0))
        s_new = dec * s_in + ds
        on = o * lax.rsqrt(jnp.mean(o * o, axis=-1, keepdims=True) + EPS) * gn
        o_ref[sl, :] = (on * sg_ref[sl, :].astype(F32)).astype(o_ref.dtype)
        if nseq == 1:
            s_run = s_new
        else:
            st_ref[rb * nseq:(rb + 1) * nseq] = s_new.reshape(nseq, HEAD_DIM, HEAD_DIM)

    if nseq == 1:
        s_scr[...] = s_run

        @pl.when(t_idx == nt - 1)
        def _():
            st_ref[...] = s_run.reshape(st_ref.shape)


def _level_consts(tc):
    R = ROW_BLOCK
    t = np.arange(R)[:, None]
    s = np.arange(R)[None, :]
    cum = ((s <= t) & (t // tc == s // tc)).astype(np.float32)
    x = t ^ s
    masks = [t == s]
    m = 1
    while m < tc:
        masks.append((t > s) & (x >= m) & (x < 2 * m))
        m *= 2
    return jnp.asarray(cum, BF16), jnp.asarray(np.stack(masks), F32)


def _hgrn_scan(q, k, v, b, sg, gnorm, masks, state, layer_j, *, bsz, seq):
    nh, m, _ = q.shape
    R = ROW_BLOCK
    has_state = state is not None
    if seq >= R:
        tc, nseq = R, 1
        tt = min(HGRN_TT, seq)
        nt = seq // tt
        nb = 1
        grid = (nh, bsz, nt)
        row_map = lambda h, b, t: (h, b * nt + t, 0)
    else:
        assert has_state
        tc, nseq = seq, R // seq
        tt = min(HGRN_TT, m)
        nt = 1
        nb = tt // seq
        grid = (nh, m // tt, 1)
        row_map = lambda h, b, t: (h, b, 0)
    rspec = pl.BlockSpec((None, tt, HEAD_DIM), row_map)
    const2 = lambda h, b, t: (0, 0)
    in_specs = [rspec] * 5 + [pl.BlockSpec((1, HEAD_DIM), const2), pl.BlockSpec(masks.shape, lambda h, b, t: (0, 0, 0))]
    args = [q, k, v, b, sg, gnorm, masks]
    if has_state:
        in_specs.append(pl.BlockSpec((None, nb, None, HEAD_DIM, HEAD_DIM), lambda h, b, t: (layer_j, b, h, 0, 0)))
        args.append(state)
    return pl.pallas_call(
        functools.partial(_hgrn_scan_kernel, tc=tc, nseq=nseq, nrb=tt // R, nt=nt, has_state=has_state),
        grid=grid,
        in_specs=in_specs,
        out_specs=[rspec, pl.BlockSpec((nb, None, HEAD_DIM, HEAD_DIM), lambda h, b, t: (b, h, 0, 0))],
        out_shape=[jax.ShapeDtypeStruct((nh, m, HEAD_DIM), BF16),
                   jax.ShapeDtypeStruct((bsz, nh, HEAD_DIM, HEAD_DIM), F32)],
        scratch_shapes=[pltpu.VMEM((HEAD_DIM, HEAD_DIM), F32)],
        compiler_params=_cparams(("parallel", "parallel", "arbitrary")),
        name="hgrn_scan",
    )(*args)


def _hgrn_out_kernel(x_ref, og_ref, w_ref, out_ref, *, nh):
    og = jnp.concatenate([og_ref[hh] for hh in range(nh)], axis=-1)
    out_ref[...] = x_ref[...] + jnp.dot(og, w_ref[...], preferred_element_type=F32)


def _hgrn_out(x, og, w_out, layer_j):
    m, d = x.shape
    nh = og.shape[0]
    tm = min(1024, m)
    return pl.pallas_call(
        functools.partial(_hgrn_out_kernel, nh=nh),
        grid=(m // tm,),
        in_specs=[
            pl.BlockSpec((tm, d), lambda i: (i, 0)),
            pl.BlockSpec((nh, tm, HEAD_DIM), lambda i: (0, i, 0)),
            pl.BlockSpec((None, d, d), lambda i: (layer_j, 0, 0)),
        ],
        out_specs=pl.BlockSpec((tm, d), lambda i: (i, 0)),
        out_shape=jax.ShapeDtypeStruct((m, d), F32),
        compiler_params=_cparams(("parallel",)),
        name="hgrn_out",
    )(x, og, w_out)


def _split3(a):
    hi = a.astype(BF16)
    r1 = a - hi.astype(F32)
    mid = r1.astype(BF16)
    lo = (r1 - mid.astype(F32)).astype(BF16)
    return hi, mid, lo


def _dot_precise(a, b):
    ah, am, al = _split3(a)
    bh, bm, bl = _split3(b)
    d = functools.partial(jnp.dot, preferred_element_type=F32)
    return (d(ah, bh) + (d(ah, bm) + d(am, bh))) + ((d(am, bm) + d(ah, bl)) + d(al, bh))


def _s5_prep_kernel(arc_ref, aic_ref, arr_ref, air_ref, ldt_ref, x1_ref, x2_ref, y1_ref, y2_ref, plt_ref, pls_ref,
                    w1_ref, w2_ref, w3_ref, pw_ref):
    L, GC, P, GL = S5_L, S5_GC, S5_P, S5_GL
    W = L * GC
    TW = L * GL * GC
    lane_t = lax.broadcasted_iota(jnp.int32, (1, W), 1) // GC
    row_s = lax.broadcasted_iota(jnp.int32, (W, 1), 0) // GC
    sgn_l = jnp.where(lax.broadcasted_iota(jnp.int32, (1, 2 * P), 1) < P, -1.0, 1.0)
    sgn_s = jnp.where(lax.broadcasted_iota(jnp.int32, (2 * P, 1), 0) < P, 1.0, -1.0)
    row16 = lax.broadcasted_iota(jnp.int32, (16, 1), 0)
    dot = functools.partial(jnp.dot, preferred_element_type=F32)

    def cmul(ar, ai, br, bi):
        return ar * br - ai * bi, ar * bi + ai * br

    bases = []
    pw = jnp.zeros((16, GL * 2 * P), F32)
    for g in range(GL):
        dt = jnp.exp(ldt_ref[g])
        a_re_c, a_im_c = arc_ref[g], aic_ref[g]
        er = jnp.exp(a_re_c * dt)
        lbr_c, lbi_c = er * jnp.cos(a_im_c * dt), er * jnp.sin(a_im_c * dt)
        a_re_r, a_im_r = arr_ref[g], air_ref[g]
        er_r = jnp.exp(a_re_r * dt)
        lbr_r, lbi_r = er_r * jnp.cos(a_im_r * dt), er_r * jnp.sin(a_im_r * dt)
        den = a_re_r * a_re_r + a_im_r * a_im_r
        nr, ni = lbr_r - 1.0, lbi_r
        f_r = (nr * a_re_r + ni * a_im_r) / den
        f_i = (ni * a_re_r - nr * a_im_r) / den

        pc = [(jnp.ones_like(lbr_c), jnp.zeros_like(lbr_c))]
        pr = [(jnp.ones_like(lbr_r), jnp.zeros_like(lbr_r))]
        for _ in range(L):
            pc.append(cmul(pc[-1][0], pc[-1][1], lbr_c, lbi_c))
            pr.append(cmul(pr[-1][0], pr[-1][1], lbr_r, lbi_r))

        q_re = jnp.zeros((2 * P, W), F32)
        q_im = jnp.zeros((2 * P, W), F32)
        for t in range(L):
            sel = lane_t == t
            q_re = jnp.where(sel, pc[t][0], q_re)
            q_im = jnp.where(sel, pc[t][1], q_im)
        y1, y2 = y1_ref[g], y2_ref[g]
        mp = sgn_s * (q_re * y1) - q_im * y2
        q1_re, q1_im = cmul(q_re, q_im, lbr_c, lbi_c)
        w3g = (sgn_s * (q1_re * y1) - q1_im * y2).astype(BF16)
        z3 = dot(w3g, plt_ref[g]).astype(BF16)
        for ri in range(2):
            w3_ref[ri * GL * P + g * P:ri * GL * P + (g + 1) * P, :] = z3[ri * P:(ri + 1) * P]

        cf_re = jnp.zeros((W, 2 * P), F32)
        cf_im = jnp.zeros((W, 2 * P), F32)
        for s in range(L):
            sel = row_s == s
            cf_re = jnp.where(sel, pr[L - 1 - s][0], cf_re)
            cf_im = jnp.where(sel, pr[L - 1 - s][1], cf_im)
        cf_re, cf_im = cmul(cf_re, cf_im, f_r, f_i)
        x1, x2 = x1_ref[g], x2_ref[g]
        w2g = (cf_re * x1 + (cf_im * sgn_l) * x2).astype(BF16)
        v2 = dot(w2g, pls_ref[g]).astype(BF16)
        for s in range(L):
            w2_ref[s * GL * GC + g * GC:s * GL * GC + (g + 1) * GC, :] = v2[s * GC:(s + 1) * GC]

        bbp = f_r * x1[0:GC] + (f_i * sgn_l) * x2[0:GC]
        base = _dot_precise(bbp, mp).astype(BF16)
        bases.append(dot(base, plt_ref[g]))

        tab = jnp.zeros((16, 2 * P), F32)
        cur = pr[L]
        for k in range(8):
            tab = jnp.where(row16 == k, cur[0], tab)
            tab = jnp.where(row16 == 8 + k, cur[1] * sgn_l, tab)
            cur = cmul(cur[0], cur[1], cur[0], cur[1])
        th, tm_, tl = _split3(tab)
        pls = pls_ref[g]
        pw = pw + ((dot(th, pls) + dot(tm_, pls)) + dot(tl, pls))

    base8 = jnp.concatenate(bases, axis=0)
    lane = lax.broadcasted_iota(jnp.int32, (1, TW), 1)
    blk = GL * GC
    w1_ref[0:blk, :] = base8.astype(BF16)
    for s in range(1, L):
        w1_ref[s * blk:(s + 1) * blk, :] = jnp.where(lane >= s * blk, pltpu.roll(base8, s * blk, 1), 0.0).astype(BF16)
    pw_ref[...] = pw


def _s5_placements():
    L, GC, P, GL = S5_L, S5_GC, S5_P, S5_GL
    plt = np.zeros((GL, L * GC, L * GL * GC), np.float32)
    pls = np.zeros((GL, 2 * P, 2 * GL * P), np.float32)
    for gl in range(GL):
        for t in range(L):
            for c in range(GC):
                plt[gl, t * GC + c, t * GL * GC + gl * GC + c] = 1.0
        for ri in range(2):
            for p in range(P):
                pls[gl, ri * P + p, ri * GL * P + gl * P + p] = 1.0
    return jnp.asarray(plt, BF16), jnp.asarray(pls, BF16)


def _s5_prep(a_re, a_im, log_dt, b_re, b_im, c_re, c_im):
    g, p = a_re.shape
    L, GC, GL = S5_L, S5_GC, S5_GL
    W = L * GC
    TW = L * GL * GC
    SW = 2 * GL * p
    nblk = g // GL
    a_re2 = jnp.concatenate([a_re, a_re], axis=1)
    a_im2 = jnp.concatenate([a_im, a_im], axis=1)
    arc, aic = a_re2[:, :, None], a_im2[:, :, None]
    arr, air = a_re2[:, None, :], a_im2[:, None, :]
    ldt = log_dt[:, None, None]
    btr, bti = jnp.transpose(b_re, (0, 2, 1)), jnp.transpose(b_im, (0, 2, 1))
    x1 = jnp.tile(jnp.concatenate([btr, bti], axis=2), (1, L, 1))
    x2 = jnp.tile(jnp.concatenate([bti, btr], axis=2), (1, L, 1))
    ctr, cti = jnp.transpose(c_re, (0, 2, 1)), jnp.transpose(c_im, (0, 2, 1))
    y1 = jnp.tile(jnp.concatenate([ctr, cti], axis=1), (1, 1, L))
    y2 = jnp.tile(jnp.concatenate([cti, ctr], axis=1), (1, 1, L))
    plt, pls = _s5_placements()

    def spec(shape):
        return pl.BlockSpec((GL,) + shape, lambda i: (i, 0, 0))

    const3 = lambda i: (0, 0, 0)
    out3 = lambda i: (i, 0, 0)
    return pl.pallas_call(
        _s5_prep_kernel,
        grid=(nblk,),
        in_specs=[spec((2 * p, 1)), spec((2 * p, 1)), spec((1, 2 * p)), spec((1, 2 * p)), spec((1, 1)),
                  spec((W, 2 * p)), spec((W, 2 * p)), spec((2 * p, W)), spec((2 * p, W)),
                  pl.BlockSpec(plt.shape, const3), pl.BlockSpec(pls.shape, const3)],
        out_specs=[pl.BlockSpec((None, TW, TW), out3), pl.BlockSpec((None, TW, SW), out3),
                   pl.BlockSpec((None, SW, TW), out3), pl.BlockSpec((None, 16, SW), out3)],
        out_shape=[jax.ShapeDtypeStruct((nblk, TW, TW), BF16), jax.ShapeDtypeStruct((nblk, TW, SW), BF16),
                   jax.ShapeDtypeStruct((nblk, SW, TW), BF16), jax.ShapeDtypeStruct((nblk, 16, SW), F32)],
        compiler_params=_cparams(("parallel",)),
        name="s5_prep",
    )(arc, aic, arr, air, ldt, x1, x2, y1, y2, plt, pls)


def _s5_core_kernel(*refs, nrows, has_state):
    if has_state:
        hn_ref, w1_ref, w2_ref, w3_ref, pw_ref, s0_ref, y_ref, fin_ref = refs
    else:
        hn_ref, w1_ref, w2_ref, w3_ref, pw_ref, y_ref, fin_ref = refs
    L = S5_L
    half = S5_GL * S5_P
    dot = functools.partial(jnp.dot, preferred_element_type=F32)

    def cscale(s, k):
        return s * pw_ref[k:k + 1, :] + pltpu.roll(s, half, 1) * pw_ref[8 + k:9 + k, :]

    u = jnp.concatenate([hn_ref[pl.ds(s, nrows, stride=L), :] for s in range(L)], axis=1).astype(BF16)
    x = dot(u, w2_ref[...])
    if has_state:
        s_in = s0_ref[...]
        fin_ref[...] = x + cscale(s_in, 0)
    else:
        row = lax.broadcasted_iota(jnp.int32, (nrows, 1), 0)
        sh, k = 1, 0
        while sh < nrows:
            shifted = jnp.where(row >= sh, pltpu.roll(x, sh, 0), 0.0)
            x = x + cscale(shifted, k)
            sh *= 2
            k += 1
        s_in = jnp.where(row >= 1, pltpu.roll(x, 1, 0), 0.0)
        fin_ref[...] = x[nrows - 1:nrows, :]
    y = dot(u, w1_ref[...]) + dot(s_in.astype(BF16), w3_ref[...])
    lanes = y_ref.shape[1]
    for t in range(L):
        y_ref[pl.ds(t, nrows, stride=L), :] = y[:, t * lanes:(t + 1) * lanes]


def _s5_core(hn, w1, w2, w3, pw, s0, *, bsz, seq):
    m, d = hn.shape
    L = S5_L
    nblk, tw, sw = w2.shape
    lanes = d // nblk
    has_state = s0 is not None
    assert has_state == (seq == L) and seq % L == 0 and seq // L <= 256
    if has_state:
        tile, nb = m, 1
        fin_shape = jax.ShapeDtypeStruct((bsz, nblk * sw), F32)
        fin_spec = pl.BlockSpec((bsz, sw), lambda g, b: (0, g))
    else:
        tile, nb = seq, bsz
        fin_shape = jax.ShapeDtypeStruct((bsz, 1, nblk * sw), F32)
        fin_spec = pl.BlockSpec((None, 1, sw), lambda g, b: (b, 0, g))
    nrows = tile // L
    wmap = lambda g, b: (g, 0, 0)
    in_specs = [
        pl.BlockSpec((tile, lanes), lambda g, b: (b, g)),
        pl.BlockSpec((None, tw, tw), wmap), pl.BlockSpec((None, tw, sw), wmap),
        pl.BlockSpec((None, sw, tw), wmap), pl.BlockSpec((None, 16, sw), wmap),
    ]
    args = [hn, w1, w2, w3, pw]
    if has_state:
        in_specs.append(pl.BlockSpec((bsz, sw), lambda g, b: (0, g)))
        args.append(s0)
    return pl.pallas_call(
        functools.partial(_s5_core_kernel, nrows=nrows, has_state=has_state),
        grid=(nblk, nb),
        in_specs=in_specs,
        out_specs=[pl.BlockSpec((tile, lanes), lambda g, b: (b, g)), fin_spec],
        out_shape=[jax.ShapeDtypeStruct((m, d), F32), fin_shape],
        compiler_params=_cparams(("parallel", "parallel")),
        name="s5_core",
    )(*args)


def _gelu_tanh(x):
    return 0.5 * x * (1.0 + jnp.tanh(0.7978845608028654 * (x + 0.044715 * (x * x * x))))


def _s5_out_kernel(x_ref, ln_ref, y_ref, d_ref, w_ref, out_ref, *, d):
    x = x_ref[...]
    h = _rms(x, ln_ref[...])
    y = y_ref[...].astype(F32) + d_ref[...] * h
    z = _gelu_tanh(y).astype(BF16)
    ag = jnp.dot(z, w_ref[...], preferred_element_type=F32)
    out_ref[...] = x + ag[:, :d] * _sigmoid(ag[:, d:])


def _s5_out(x, ln, y, d_skip, w_glu, layer_j):
    m, d = x.shape
    tm = min(1024, m)
    return pl.pallas_call(
        functools.partial(_s5_out_kernel, d=d),
        grid=(m // tm,),
        in_specs=[
            pl.BlockSpec((tm, d), lambda i: (i, 0)),
            pl.BlockSpec((1, d), lambda i: (0, 0)),
            pl.BlockSpec((tm, d), lambda i: (i, 0)),
            pl.BlockSpec((1, d), lambda i: (0, 0)),
            pl.BlockSpec((None, d, 2 * d), lambda i: (layer_j, 0, 0)),
        ],
        out_specs=pl.BlockSpec((tm, d), lambda i: (i, 0)),
        out_shape=jax.ShapeDtypeStruct((m, d), F32),
        compiler_params=_cparams(("parallel",)),
        name="s5_out",
    )(x, ln, y, d_skip, w_glu)


def _unpack_state(fin, bsz):
    f = fin.reshape(bsz, -1, 2, S5_GL, S5_P)
    return f[:, :, 0].reshape(bsz, -1, S5_P), f[:, :, 1].reshape(bsz, -1, S5_P)


def _layer(i, st, wts, w_bf):
    j = i // 2
    depth = wts["ln_ffn1"].shape[0]
    x, bsz, seq = st["x"], st["bsz"], st["seq"]
    is_hgrn = i % 2 == 0

    def ffn(name, x, ln, gain2, **kw):
        key = (name, i)
        if key in w_bf:
            (w_in, w_out), lyr = w_bf[key], 0
        else:
            w_in, w_out, lyr = wts[name + "_w_in"], wts[name + "_w_out"], i
        out, hn, cast = _ffn(x, ln, w_in, w_out, lyr, gain2, **kw)
        if cast is not None:
            w_bf[key] = cast
        return out, hn

    x, hn = ffn("ffn1", x, wts["ln_ffn1"][i:i + 1], wts["ln_mix"][i:i + 1], hn_dtype=BF16 if is_hgrn else F32)
    if is_hgrn:
        cum, masks = _level_consts(min(seq, ROW_BLOCK))
        q, k, v, b, sg = _hgrn_gates(hn, wts["hgrn_w_in"], j, wts["hgrn_lb_logits"], i, cum)
        og, s_t = _hgrn_scan(q, k, v, b, sg, wts["hgrn_gnorm"][j:j + 1], masks, st["hgrn"], j, bsz=bsz, seq=seq)
        st["new_hgrn"].append(s_t)
        x = _hgrn_out(x, og, wts["hgrn_w_out"], j)
    else:
        w1, w2, w3, pw = wts["s5_mats"][j]
        s0 = None
        if st["re"] is not None:
            pack = lambda a: a[j].reshape(bsz, -1, 1, S5_GL, S5_P)
            s0 = jnp.concatenate([pack(st["re"]), pack(st["im"])], axis=2).reshape(bsz, -1)
        y, fin = _s5_core(hn, w1, w2, w3, pw, s0, bsz=bsz, seq=seq)
        f_re, f_im = _unpack_state(fin, bsz)
        st["new_re"].append(f_re)
        st["new_im"].append(f_im)
        x = _s5_out(x, wts["ln_mix"][i:i + 1], y, wts["s5_d"][j:j + 1], wts["s5_w_glu"], j)
    last = i == depth - 1
    gain2 = wts["ln_final"] if last else wts["ln_ffn2"][i:i + 1]
    x, _ = ffn("ffn2", x, wts["ln_ffn2"][i:i + 1], gain2, final_norm=last)
    st["x"] = x


def kernel(x_prompt, x_sample, state_hgrn, state_s5_re, state_s5_im, ln_ffn1, ffn1_w_in, ffn1_w_out, ln_mix, ln_ffn2, ffn2_w_in, ffn2_w_out, hgrn_lb_logits, hgrn_w_in, hgrn_gnorm, hgrn_w_out, s5_a_re, s5_a_im, s5_log_dt, s5_b_re, s5_b_im, s5_c_re, s5_c_im, s5_d, s5_w_glu, ln_final):
    depth, d = ln_ffn1.shape
    wts = dict(
        ln_ffn1=ln_ffn1, ln_mix=ln_mix, ln_ffn2=ln_ffn2, ln_final=ln_final.reshape(1, -1),
        ffn1_w_in=ffn1_w_in, ffn1_w_out=ffn1_w_out, ffn2_w_in=ffn2_w_in, ffn2_w_out=ffn2_w_out,
        hgrn_lb_logits=hgrn_lb_logits, hgrn_w_in=hgrn_w_in.astype(BF16), hgrn_gnorm=hgrn_gnorm,
        hgrn_w_out=hgrn_w_out.astype(BF16), s5_d=s5_d, s5_w_glu=s5_w_glu.astype(BF16),
        s5_mats=[_s5_prep(s5_a_re[j], s5_a_im[j], s5_log_dt[j], s5_b_re[j], s5_b_im[j], s5_c_re[j], s5_c_im[j])
                 for j in range(s5_a_re.shape[0])],
    )

    def stream(x3, hgrn, re, im):
        bsz, seq, _ = x3.shape
        return dict(x=x3.reshape(bsz * seq, d), bsz=bsz, seq=seq, hgrn=hgrn, re=re, im=im,
                    new_hgrn=[], new_re=[], new_im=[])

    sample = stream(x_sample, state_hgrn, state_s5_re, state_s5_im)
    prompt = stream(x_prompt, None, None, None)
    w_bf = {}
    for i in range(depth):
        _layer(i, sample, wts, w_bf)
        _layer(i, prompt, wts, w_bf)

    def outs(st, x3):
        return (st["x"].reshape(x3.shape), jnp.stack(st["new_hgrn"]), jnp.stack(st["new_re"]), jnp.stack(st["new_im"]))

    y_p, hg_p, re_p, im_p = outs(prompt, x_prompt)
    y_s, hg_s, re_s, im_s = outs(sample, x_sample)
    return (y_p, y_s, hg_p, re_p, im_p, hg_s, re_s, im_s)
```

```python
import functools

import numpy as np
import jax
import jax.numpy as jnp
from jax import lax
from jax.experimental import pallas as pl
from jax.experimental.pallas import tpu as pltpu

F32 = jnp.float32
BF16 = jnp.bfloat16

EPS = 1e-6
GATE_FLOOR = 1e-30
MACARON_SCALE = 0.5
LB_CLIP = 1.0 - 1e-4

HEAD_DIM = 128
S5_GC = 16
S5_P = 64
S5_L = 8
S5_GL = 8
ROW_BLOCK = 128
FFN_TM = 1024
FFN_TF = 256
HGRN_TT = 512
VMEM_LIMIT = 52 * 1024 * 1024


def _cparams(sem):
    return pltpu.CompilerParams(dimension_semantics=sem, vmem_limit_bytes=VMEM_LIMIT)


def _rms(x, gain):
    return x * lax.rsqrt(jnp.mean(x * x, axis=-1, keepdims=True) + EPS) * gain


def _sigmoid(x):
    return 1.0 / (1.0 + jnp.exp(-x))


def _ffn_cast_kernel(x_ref, ln_ref, wa_ref, wb_ref, wout_ref, g2_ref, *rest, nf, hn_dtype, final_norm):
    rest = list(rest)
    out_ref = rest.pop(0)
    hn_ref = rest.pop(0) if hn_dtype is not None else None
    wa_bf_ref, wb_bf_ref, wout_bf_ref, h_scr, acc_scr = rest
    j = pl.program_id(1)

    @pl.when(j == 0)
    def _():
        h_scr[...] = _rms(x_ref[...], ln_ref[...]).astype(BF16)
        acc_scr[...] = jnp.zeros_like(acc_scr)

    wa, wb, wo = wa_ref[...].astype(BF16), wb_ref[...].astype(BF16), wout_ref[...].astype(BF16)
    wa_bf_ref[...] = wa
    wb_bf_ref[...] = wb
    wout_bf_ref[...] = wo
    h = h_scr[...]
    a = jnp.dot(h, wa, preferred_element_type=F32)
    b = jnp.dot(h, wb, preferred_element_type=F32)
    g = (a * _sigmoid(a) * b).astype(BF16)
    acc_scr[...] += jnp.dot(g, wo, preferred_element_type=F32)

    @pl.when(j == nf - 1)
    def _():
        y = x_ref[...] + MACARON_SCALE * acc_scr[...]
        if final_norm:
            out_ref[...] = _rms(y, g2_ref[...])
        else:
            out_ref[...] = y
        if hn_ref is not None:
            hn_ref[...] = _rms(y, g2_ref[...]).astype(hn_ref.dtype)


def _ffn_resident_kernel(x_ref, ln_ref, wa_ref, wb_ref, wout_ref, g2_ref, *rest, tf, hn_dtype, final_norm):
    rest = list(rest)
    out_ref = rest.pop(0)
    hn_ref = rest.pop(0) if hn_dtype is not None else None
    g_scr, = rest
    x = x_ref[...]
    h = _rms(x, ln_ref[...]).astype(BF16)
    for j in range(g_scr.shape[1] // tf):
        cols = slice(j * tf, (j + 1) * tf)
        a = jnp.dot(h, wa_ref[:, cols], preferred_element_type=F32)
        b = jnp.dot(h, wb_ref[:, cols], preferred_element_type=F32)
        g_scr[:, cols] = (a * _sigmoid(a) * b).astype(BF16)
    y = x + MACARON_SCALE * jnp.dot(g_scr[...], wout_ref[...], preferred_element_type=F32)
    if final_norm:
        out_ref[...] = _rms(y, g2_ref[...])
    else:
        out_ref[...] = y
    if hn_ref is not None:
        hn_ref[...] = _rms(y, g2_ref[...]).astype(hn_ref.dtype)


def _ffn_resident(x, ln, w_in, w_out, gain2, *, hn_dtype, final_norm):
    m, d = x.shape
    wa, wb = w_in
    f = w_out.shape[1]
    tm = min(FFN_TM, m)
    row = lambda i: (i, 0)
    once = pl.Buffered(1)
    out_shape = [jax.ShapeDtypeStruct((m, d), F32)]
    out_specs = [pl.BlockSpec((tm, d), row)]
    if hn_dtype is not None:
        out_shape.append(jax.ShapeDtypeStruct((m, d), hn_dtype))
        out_specs.append(pl.BlockSpec((tm, d), row))
    res = pl.pallas_call(
        functools.partial(_ffn_resident_kernel, tf=FFN_TF, hn_dtype=hn_dtype, final_norm=final_norm),
        grid=(m // tm,),
        in_specs=[
            pl.BlockSpec((tm, d), row),
            pl.BlockSpec((1, d), lambda i: (0, 0)),
            pl.BlockSpec((None, d, f), lambda i: (0, 0, 0), pipeline_mode=once),
            pl.BlockSpec((None, d, f), lambda i: (0, 0, 0), pipeline_mode=once),
            pl.BlockSpec((None, f, d), lambda i: (0, 0, 0), pipeline_mode=once),
            pl.BlockSpec((1, d), lambda i: (0, 0)),
        ],
        out_specs=out_specs,
        out_shape=out_shape,
        scratch_shapes=[pltpu.VMEM((tm, f), BF16)],
        compiler_params=_cparams(("parallel",)),
        name="ffn",
    )(x, ln, wa, wb, w_out, gain2)
    return res[0], (res[1] if hn_dtype is not None else None), None


def _ffn_cast(x, ln, w_in, w_out, layer, gain2, *, hn_dtype, final_norm):
    m, d = x.shape
    f = w_out.shape[1]
    tf = FFN_TF
    nf = f // tf
    assert m <= FFN_TM
    row = lambda i, j: (0, 0)
    out_shape = [jax.ShapeDtypeStruct((m, d), F32)]
    out_specs = [pl.BlockSpec((m, d), row)]
    if hn_dtype is not None:
        out_shape.append(jax.ShapeDtypeStruct((m, d), hn_dtype))
        out_specs.append(pl.BlockSpec((m, d), row))
    out_shape += [jax.ShapeDtypeStruct((1, d, f), BF16)] * 2 + [jax.ShapeDtypeStruct((1, f, d), BF16)]
    out_specs += [pl.BlockSpec((None, d, tf), lambda i, j: (0, 0, j))] * 2
    out_specs += [pl.BlockSpec((None, tf, d), lambda i, j: (0, j, 0))]
    res = pl.pallas_call(
        functools.partial(_ffn_cast_kernel, nf=nf, hn_dtype=hn_dtype, final_norm=final_norm),
        grid=(1, nf),
        in_specs=[
            pl.BlockSpec((m, d), row),
            pl.BlockSpec((1, d), row),
            pl.BlockSpec((None, d, tf), lambda i, j: (layer, 0, j)),
            pl.BlockSpec((None, d, tf), lambda i, j: (layer, 0, nf + j)),
            pl.BlockSpec((None, tf, d), lambda i, j: (layer, j, 0)),
            pl.BlockSpec((1, d), row),
        ],
        out_specs=out_specs,
        out_shape=out_shape,
        scratch_shapes=[pltpu.VMEM((m, d), BF16), pltpu.VMEM((m, d), F32)],
        compiler_params=_cparams(("arbitrary", "arbitrary")),
        name="ffn_cast",
    )(x, ln, w_in, w_in, w_out, gain2)
    res = list(res)
    out = res.pop(0)
    hn = res.pop(0) if hn_dtype is not None else None
    return out, hn, ((res[0], res[1]), res[2])


def _hgrn_gates_kernel(hn_ref, w_ref, lbl_ref, cum_ref, q_ref, k_ref, v_ref, b_ref, sg_ref, *, layer, d, nh):
    h = hn_ref[...]
    logits = lbl_ref[...]
    e = jnp.exp(logits - jnp.max(logits, axis=0, keepdims=True))
    p = e / jnp.sum(e, axis=0, keepdims=True)
    cum = p[0:1]
    for r in range(1, layer + 1):
        cum = cum + p[r:r + 1]
    lb = jnp.clip(cum - p[0:1], 0.0, LB_CLIP)

    def heads(ref, val):
        for hh in range(nh):
            ref[hh] = val[:, hh * HEAD_DIM:(hh + 1) * HEAD_DIM].astype(ref.dtype)

    pq = jnp.dot(h, w_ref[:, 0:d], preferred_element_type=F32)
    heads(q_ref, pq)
    pf = jnp.dot(h, w_ref[:, d:2 * d], preferred_element_type=F32)
    ez = jnp.exp(-jnp.abs(pf))
    r = 1.0 / (1.0 + ez)
    pos = pf >= 0.0
    sig_p = jnp.where(pos, r, ez * r)
    sig_n = jnp.where(pos, ez * r, r)
    fg = lb + (1.0 - lb) * sig_p
    hi, mid, lo = _split3(jnp.log(jnp.maximum(fg, GATE_FLOOR)))
    cum = cum_ref[...]
    dot = functools.partial(jnp.dot, preferred_element_type=F32)
    R = ROW_BLOCK
    for rb in range(h.shape[0] // R):
        rows = slice(rb * R, (rb + 1) * R)
        b = (dot(cum, hi[rows]) + dot(cum, mid[rows])) + dot(cum, lo[rows])
        for hh in range(nh):
            b_ref[hh, rows, :] = b[:, hh * HEAD_DIM:(hh + 1) * HEAD_DIM]
    heads(k_ref, (1.0 - lb) * sig_n)
    pv = jnp.dot(h, w_ref[:, 2 * d:3 * d], preferred_element_type=F32)
    heads(v_ref, pv)
    pg = jnp.dot(h, w_ref[:, 3 * d:4 * d], preferred_element_type=F32)
    heads(sg_ref, pg * _sigmoid(pg))


def _hgrn_gates(hn, w_in, layer_j, lb_logits, layer, cum):
    m, d = hn.shape
    nh = d // HEAD_DIM
    tm = min(512, m)
    hm = lambda dt: jax.ShapeDtypeStruct((nh, m, HEAD_DIM), dt)
    hspec = pl.BlockSpec((nh, tm, HEAD_DIM), lambda i: (0, i, 0))
    return pl.pallas_call(
        functools.partial(_hgrn_gates_kernel, layer=layer, d=d, nh=nh),
        grid=(m // tm,),
        in_specs=[
            pl.BlockSpec((tm, d), lambda i: (i, 0)),
            pl.BlockSpec((None, d, 4 * d), lambda i: (layer_j, 0, 0)),
            pl.BlockSpec(lb_logits.shape, lambda i: (0, 0)),
            pl.BlockSpec(cum.shape, lambda i: (0, 0)),
        ],
        out_specs=[hspec] * 5,
        out_shape=[hm(BF16), hm(BF16), hm(BF16), hm(F32), hm(BF16)],
        compiler_params=_cparams(("parallel",)),
        name="hgrn_gates",
    )(hn, w_in, lb_logits, cum)


def _anchor(b, m):
    R, n = b.shape
    if m >= 8:
        parts = [jnp.broadcast_to(b[blk * 2 * m + m - 1:blk * 2 * m + m, :], (2 * m, n)) for blk in range(R // (2 * m))]
        return parts[0] if len(parts) == 1 else jnp.concatenate(parts, axis=0)
    if m == 1:
        odd = jnp.bitwise_and(lax.broadcasted_iota(jnp.int32, (R, 1), 0), 1) == 1
        return jnp.where(odd, pltpu.roll(b, 1, 0), b)
    b3 = b.reshape(R // 8, 8, n)
    if m == 4:
        a3 = jnp.broadcast_to(b3[:, 3:4, :], b3.shape)
    else:
        sub = lax.broadcasted_iota(jnp.int32, (1, 8, 1), 1)
        a3 = jnp.where(sub < 4, jnp.broadcast_to(b3[:, 1:2, :], b3.shape), jnp.broadcast_to(b3[:, 5:6, :], b3.shape))
    return a3.reshape(R, n)


def _block_end(b, tc):
    R, n = b.shape
    if tc == 8:
        b3 = b.reshape(R // 8, 8, n)
        return jnp.broadcast_to(b3[:, 7:8, :], b3.shape).reshape(R, n)
    parts = [jnp.broadcast_to(b[blk * tc + tc - 1:blk * tc + tc, :], (tc, n)) for blk in range(R // tc)]
    return parts[0] if len(parts) == 1 else jnp.concatenate(parts, axis=0)


def _hgrn_scan_kernel(*refs, tc, nseq, nrb, nt, has_state):
    if has_state:
        q_ref, k_ref, v_ref, b_ref, sg_ref, gn_ref, msk_ref, s0_ref, o_ref, st_ref, s_scr = refs
    else:
        q_ref, k_ref, v_ref, b_ref, sg_ref, gn_ref, msk_ref, o_ref, st_ref, s_scr = refs
    R = ROW_BLOCK
    t_idx = pl.program_id(2)
    dot = functools.partial(jnp.dot, preferred_element_type=F32)
    dn = (((1,), (1,)), ((), ()))

    if nseq == 1:
        @pl.when(t_idx == 0)
        def _():
            if has_state:
                s_scr[...] = s0_ref[...].reshape(s_scr.shape)
            else:
                s_scr[...] = jnp.zeros_like(s_scr)

    seq_of_row = lax.broadcasted_iota(jnp.int32, (R, 1), 0) // tc
    gn = gn_ref[...]

    def local_part(rb):
        sl = pl.ds(rb * R, R)
        q = q_ref[sl, :].astype(F32)
        kk = k_ref[sl, :].astype(F32)
        vb = v_ref[sl, :]
        b = b_ref[sl, :]
        scores = lax.dot_general(q_ref[sl, :], k_ref[sl, :], dn, preferred_element_type=F32) * msk_ref[0]
        m, level = 1, 1
        while m < tc:
            w = jnp.exp(-jnp.abs(b - _anchor(b, m)))
            s_m = lax.dot_general((q * w).astype(BF16), (kk * w).astype(BF16), dn, preferred_element_type=F32)
            scores = scores + s_m * msk_ref[level]
            m *= 2
            level += 1
        e_end = _block_end(b, tc)
        qe = q * jnp.exp(b)
        ke = kk * jnp.exp(e_end - b)
        if nseq > 1:
            qe = jnp.concatenate([jnp.where(seq_of_row == i, qe, 0.0) for i in range(nseq)], axis=1)
            ke = jnp.concatenate([jnp.where(seq_of_row == i, ke, 0.0) for i in range(nseq)], axis=1)
        ds = lax.dot_general(ke.astype(BF16), vb, (((0,), (0,)), ((), ())), preferred_element_type=F32)
        decs = []
        for i in range(nseq):
            dec_row = jnp.exp(e_end[i * tc:i * tc + 1, :])
            decs.append(jnp.transpose(jnp.broadcast_to(dec_row, (HEAD_DIM, HEAD_DIM))))
        dec = decs[0] if nseq == 1 else jnp.concatenate(decs, axis=0)
        return jnp.concatenate([scores.astype(BF16), qe.astype(BF16)], axis=1), vb, ds, dec

    parts = [local_part(rb) for rb in range(nrb)]

    s_run = s_scr[...] if nseq == 1 else None
    for rb in range(nrb):
        sq, vb, ds, dec = parts[rb]
        sl = pl.ds(rb * R, R)
        if nseq == 1:
            s_in = s_run
        else:
            s_in = s0_ref[rb * nseq:(rb + 1) * nseq].reshape(nseq * HEAD_DIM, HEAD_DIM)
        o = dot(sq, jnp.concatenate([vb, s_in.astype(BF16)], axis=0))
        s_new = dec * s_in + ds
        on = o * lax.rsqrt(jnp.mean(o * o, axis=-1, keepdims=True) + EPS) * gn
        o_ref[sl, :] = (on * sg_ref[sl, :].astype(F32)).astype(o_ref.dtype)
        if nseq == 1:
            s_run = s_new
        else:
            st_ref[rb * nseq:(rb + 1) * nseq] = s_new.reshape(nseq, HEAD_DIM, HEAD_DIM)

    if nseq == 1:
        s_scr[...] = s_run

        @pl.when(t_idx == nt - 1)
        def _():
            st_ref[...] = s_run.reshape(st_ref.shape)


def _level_consts(tc):
    R = ROW_BLOCK
    t = np.arange(R)[:, None]
    s = np.arange(R)[None, :]
    cum = ((s <= t) & (t // tc == s // tc)).astype(np.float32)
    x = t ^ s
    masks = [t == s]
    m = 1
    while m < tc:
        masks.append((t > s) & (x >= m) & (x < 2 * m))
        m *= 2
    return jnp.asarray(cum, BF16), jnp.asarray(np.stack(masks), F32)


def _hgrn_scan(q, k, v, b, sg, gnorm, masks, state, layer_j, *, bsz, seq):
    nh, m, _ = q.shape
    R = ROW_BLOCK
    has_state = state is not None
    if seq >= R:
        tc, nseq = R, 1
        tt = min(HGRN_TT, seq)
        nt = seq // tt
        nb = 1
        grid = (nh, bsz, nt)
        row_map = lambda h, b, t: (h, b * nt + t, 0)
    else:
        assert has_state
        tc, nseq = seq, R // seq
        tt = min(HGRN_TT, m)
        nt = 1
        nb = tt // seq
        grid = (nh, m // tt, 1)
        row_map = lambda h, b, t: (h, b, 0)
    rspec = pl.BlockSpec((None, tt, HEAD_DIM), row_map)
    const2 = lambda h, b, t: (0, 0)
    in_specs = [rspec] * 5 + [pl.BlockSpec((1, HEAD_DIM), const2), pl.BlockSpec(masks.shape, lambda h, b, t: (0, 0, 0))]
    args = [q, k, v, b, sg, gnorm, masks]
    if has_state:
        in_specs.append(pl.BlockSpec((None, nb, None, HEAD_DIM, HEAD_DIM), lambda h, b, t: (layer_j, b, h, 0, 0)))
        args.append(state)
    return pl.pallas_call(
        functools.partial(_hgrn_scan_kernel, tc=tc, nseq=nseq, nrb=tt // R, nt=nt, has_state=has_state),
        grid=grid,
        in_specs=in_specs,
        out_specs=[rspec, pl.BlockSpec((nb, None, HEAD_DIM, HEAD_DIM), lambda h, b, t: (b, h, 0, 0))],
        out_shape=[jax.ShapeDtypeStruct((nh, m, HEAD_DIM), BF16),
                   jax.ShapeDtypeStruct((bsz, nh, HEAD_DIM, HEAD_DIM), F32)],
        scratch_shapes=[pltpu.VMEM((HEAD_DIM, HEAD_DIM), F32)],
        compiler_params=_cparams(("parallel", "parallel", "arbitrary")),
        name="hgrn_scan",
    )(*args)


def _hgrn_out_kernel(x_ref, og_ref, w_ref, out_ref, *, nh):
    og = jnp.concatenate([og_ref[hh] for hh in range(nh)], axis=-1)
    out_ref[...] = x_ref[...] + jnp.dot(og, w_ref[...], preferred_element_type=F32)


def _hgrn_out(x, og, w_out, layer_j):
    m, d = x.shape
    nh = og.shape[0]
    tm = min(1024, m)
    return pl.pallas_call(
        functools.partial(_hgrn_out_kernel, nh=nh),
        grid=(m // tm,),
        in_specs=[
            pl.BlockSpec((tm, d), lambda i: (i, 0)),
            pl.BlockSpec((nh, tm, HEAD_DIM), lambda i: (0, i, 0)),
            pl.BlockSpec((None, d, d), lambda i: (layer_j, 0, 0)),
        ],
        out_specs=pl.BlockSpec((tm, d), lambda i: (i, 0)),
        out_shape=jax.ShapeDtypeStruct((m, d), F32),
        compiler_params=_cparams(("parallel",)),
        name="hgrn_out",
    )(x, og, w_out)


def _split3(a):
    hi = a.astype(BF16)
    r1 = a - hi.astype(F32)
    mid = r1.astype(BF16)
    lo = (r1 - mid.astype(F32)).astype(BF16)
    return hi, mid, lo


def _dot_precise(a, b):
    ah, am, al = _split3(a)
    bh, bm, bl = _split3(b)
    d = functools.partial(jnp.dot, preferred_element_type=F32)
    return (d(ah, bh) + (d(ah, bm) + d(am, bh))) + ((d(am, bm) + d(ah, bl)) + d(al, bh))


def _s5_prep_kernel(arc_ref, aic_ref, arr_ref, air_ref, ldt_ref, x1_ref, x2_ref, y1_ref, y2_ref, plt_ref, pls_ref,
                    w1_ref, w2_ref, w3_ref, pw_ref):
    L, GC, P, GL = S5_L, S5_GC, S5_P, S5_GL
    W = L * GC
    TW = L * GL * GC
    lane_t = lax.broadcasted_iota(jnp.int32, (1, W), 1) // GC
    row_s = lax.broadcasted_iota(jnp.int32, (W, 1), 0) // GC
    sgn_l = jnp.where(lax.broadcasted_iota(jnp.int32, (1, 2 * P), 1) < P, -1.0, 1.0)
    sgn_s = jnp.where(lax.broadcasted_iota(jnp.int32, (2 * P, 1), 0) < P, 1.0, -1.0)
    row16 = lax.broadcasted_iota(jnp.int32, (16, 1), 0)
    dot = functools.partial(jnp.dot, preferred_element_type=F32)

    def cmul(ar, ai, br, bi):
        return ar * br - ai * bi, ar * bi + ai * br

    bases = []
    pw = jnp.zeros((16, GL * 2 * P), F32)
    for g in range(GL):
        dt = jnp.exp(ldt_ref[g])
        a_re_c, a_im_c = arc_ref[g], aic_ref[g]
        er = jnp.exp(a_re_c * dt)
        lbr_c, lbi_c = er * jnp.cos(a_im_c * dt), er * jnp.sin(a_im_c * dt)
        a_re_r, a_im_r = arr_ref[g], air_ref[g]
        er_r = jnp.exp(a_re_r * dt)
        lbr_r, lbi_r = er_r * jnp.cos(a_im_r * dt), er_r * jnp.sin(a_im_r * dt)
        den = a_re_r * a_re_r + a_im_r * a_im_r
        nr, ni = lbr_r - 1.0, lbi_r
        f_r = (nr * a_re_r + ni * a_im_r) / den
        f_i = (ni * a_re_r - nr * a_im_r) / den

        pc = [(jnp.ones_like(lbr_c), jnp.zeros_like(lbr_c))]
        pr = [(jnp.ones_like(lbr_r), jnp.zeros_like(lbr_r))]
        for _ in range(L):
            pc.append(cmul(pc[-1][0], pc[-1][1], lbr_c, lbi_c))
            pr.append(cmul(pr[-1][0], pr[-1][1], lbr_r, lbi_r))

        q_re = jnp.zeros((2 * P, W), F32)
        q_im = jnp.zeros((2 * P, W), F32)
        for t in range(L):
            sel = lane_t == t
            q_re = jnp.where(sel, pc[t][0], q_re)
            q_im = jnp.where(sel, pc[t][1], q_im)
        y1, y2 = y1_ref[g], y2_ref[g]
        mp = sgn_s * (q_re * y1) - q_im * y2
        q1_re, q1_im = cmul(q_re, q_im, lbr_c, lbi_c)
        w3g = (sgn_s * (q1_re * y1) - q1_im * y2).astype(BF16)
        z3 = dot(w3g, plt_ref[g]).astype(BF16)
        for ri in range(2):
            w3_ref[ri * GL * P + g * P:ri * GL * P + (g + 1) * P, :] = z3[ri * P:(ri + 1) * P]

        cf_re = jnp.zeros((W, 2 * P), F32)
        cf_im = jnp.zeros((W, 2 * P), F32)
        for s in range(L):
            sel = row_s == s
            cf_re = jnp.where(sel, pr[L - 1 - s][0], cf_re)
            cf_im = jnp.where(sel, pr[L - 1 - s][1], cf_im)
        cf_re, cf_im = cmul(cf_re, cf_im, f_r, f_i)
        x1, x2 = x1_ref[g], x2_ref[g]
        w2g = (cf_re * x1 + (cf_im * sgn_l) * x2).astype(BF16)
        v2 = dot(w2g, pls_ref[g]).astype(BF16)
        for s in range(L):
            w2_ref[s * GL * GC + g * GC:s * GL * GC + (g + 1) * GC, :] = v2[s * GC:(s + 1) * GC]

        bbp = f_r * x1[0:GC] + (f_i * sgn_l) * x2[0:GC]
        base = _dot_precise(bbp, mp).astype(BF16)
        bases.append(dot(base, plt_ref[g]))

        tab = jnp.zeros((16, 2 * P), F32)
        cur = pr[L]
        for k in range(8):
            tab = jnp.where(row16 == k, cur[0], tab)
            tab = jnp.where(row16 == 8 + k, cur[1] * sgn_l, tab)
            cur = cmul(cur[0], cur[1], cur[0], cur[1])
        th, tm_, tl = _split3(tab)
        pls = pls_ref[g]
        pw = pw + ((dot(th, pls) + dot(tm_, pls)) + dot(tl, pls))

    base8 = jnp.concatenate(bases, axis=0)
    lane = lax.broadcasted_iota(jnp.int32, (1, TW), 1)
    blk = GL * GC
    w1_ref[0:blk, :] = base8.astype(BF16)
    for s in range(1, L):
        w1_ref[s * blk:(s + 1) * blk, :] = jnp.where(lane >= s * blk, pltpu.roll(base8, s * blk, 1), 0.0).astype(BF16)
    pw_ref[...] = pw


def _s5_placements():
    L, GC, P, GL = S5_L, S5_GC, S5_P, S5_GL
    plt = np.zeros((GL, L * GC, L * GL * GC), np.float32)
    pls = np.zeros((GL, 2 * P, 2 * GL * P), np.float32)
    for gl in range(GL):
        for t in range(L):
            for c in range(GC):
                plt[gl, t * GC + c, t * GL * GC + gl * GC + c] = 1.0
        for ri in range(2):
            for p in range(P):
                pls[gl, ri * P + p, ri * GL * P + gl * P + p] = 1.0
    return jnp.asarray(plt, BF16), jnp.asarray(pls, BF16)


def _s5_prep(a_re, a_im, log_dt, b_re, b_im, c_re, c_im):
    g, p = a_re.shape
    L, GC, GL = S5_L, S5_GC, S5_GL
    W = L * GC
    TW = L * GL * GC
    SW = 2 * GL * p
    nblk = g // GL
    a_re2 = jnp.concatenate([a_re, a_re], axis=1)
    a_im2 = jnp.concatenate([a_im, a_im], axis=1)
    arc, aic = a_re2[:, :, None], a_im2[:, :, None]
    arr, air = a_re2[:, None, :], a_im2[:, None, :]
    ldt = log_dt[:, None, None]
    btr, bti = jnp.transpose(b_re, (0, 2, 1)), jnp.transpose(b_im, (0, 2, 1))
    x1 = jnp.tile(jnp.concatenate([btr, bti], axis=2), (1, L, 1))
    x2 = jnp.tile(jnp.concatenate([bti, btr], axis=2), (1, L, 1))
    ctr, cti = jnp.transpose(c_re, (0, 2, 1)), jnp.transpose(c_im, (0, 2, 1))
    y1 = jnp.tile(jnp.concatenate([ctr, cti], axis=1), (1, 1, L))
    y2 = jnp.tile(jnp.concatenate([cti, ctr], axis=1), (1, 1, L))
    plt, pls = _s5_placements()

    def spec(shape):
        return pl.BlockSpec((GL,) + shape, lambda i: (i, 0, 0))

    const3 = lambda i: (0, 0, 0)
    out3 = lambda i: (i, 0, 0)
    return pl.pallas_call(
        _s5_prep_kernel,
        grid=(nblk,),
        in_specs=[spec((2 * p, 1)), spec((2 * p, 1)), spec((1, 2 * p)), spec((1, 2 * p)), spec((1, 1)),
                  spec((W, 2 * p)), spec((W, 2 * p)), spec((2 * p, W)), spec((2 * p, W)),
                  pl.BlockSpec(plt.shape, const3), pl.BlockSpec(pls.shape, const3)],
        out_specs=[pl.BlockSpec((None, TW, TW), out3), pl.BlockSpec((None, TW, SW), out3),
                   pl.BlockSpec((None, SW, TW), out3), pl.BlockSpec((None, 16, SW), out3)],
        out_shape=[jax.ShapeDtypeStruct((nblk, TW, TW), BF16), jax.ShapeDtypeStruct((nblk, TW, SW), BF16),
                   jax.ShapeDtypeStruct((nblk, SW, TW), BF16), jax.ShapeDtypeStruct((nblk, 16, SW), F32)],
        compiler_params=_cparams(("parallel",)),
        name="s5_prep",
    )(arc, aic, arr, air, ldt, x1, x2, y1, y2, plt, pls)


def _s5_core_kernel(*refs, nrows, has_state):
    if has_state:
        hn_ref, w1_ref, w2_ref, w3_ref, pw_ref, s0_ref, y_ref, fin_ref = refs
    else:
        hn_ref, w1_ref, w2_ref, w3_ref, pw_ref, y_ref, fin_ref = refs
    L = S5_L
    half = S5_GL * S5_P
    dot = functools.partial(jnp.dot, preferred_element_type=F32)

    u = jnp.concatenate([hn_ref[pl.ds(s, nrows, stride=L), :] for s in range(L)], axis=1).astype(BF16)
    x = dot(u, w2_ref[...])
    if has_state:
        s_in = s0_ref[...]
        fin_ref[...] = x + (s_in * pw_ref[0:1, :] + pltpu.roll(s_in, half, 1) * pw_ref[8:9, :])
    else:
        row = lax.broadcasted_iota(jnp.int32, (nrows, 1), 0)

        def shift(a, sh):
            if sh % 8 == 0:
                return jnp.concatenate([jnp.zeros((sh, a.shape[1]), F32), a[:nrows - sh]], axis=0)
            return jnp.where(row >= sh, pltpu.roll(a, sh, 0), 0.0)

        re_strips, im_strips = [], []
        for st in range(half // 128):
            lr = slice(st * 128, (st + 1) * 128)
            li = slice(half + st * 128, half + (st + 1) * 128)
            xr, xi = x[:, lr], x[:, li]
            sh, k = 1, 0
            while sh < nrows:
                wr, wi = pw_ref[k:k + 1, lr], pw_ref[8 + k:9 + k, li]
                sr, si = shift(xr, sh), shift(xi, sh)
                xr, xi = xr + (sr * wr - si * wi), xi + (sr * wi + si * wr)
                sh *= 2
                k += 1
            re_strips.append(xr)
            im_strips.append(xi)
        x = jnp.concatenate(re_strips + im_strips, axis=1)
        s_in = shift(x, 1)
        fin_ref[...] = x[nrows - 1:nrows, :]
    y = dot(u, w1_ref[...]) + dot(s_in.astype(BF16), w3_ref[...])
    lanes = y_ref.shape[1]
    for t in range(L):
        y_ref[pl.ds(t, nrows, stride=L), :] = y[:, t * lanes:(t + 1) * lanes]


def _s5_core(hn, w1, w2, w3, pw, s0, *, bsz, seq):
    m, d = hn.shape
    L = S5_L
    nblk, tw, sw = w2.shape
    lanes = d // nblk
    has_state = s0 is not None
    assert has_state == (seq == L) and seq % L == 0 and seq // L <= 256
    if has_state:
        tile, nb = m, 1
        fin_shape = jax.ShapeDtypeStruct((bsz, nblk * sw), F32)
        fin_spec = pl.BlockSpec((bsz, sw), lambda g, b: (0, g))
    else:
        tile, nb = seq, bsz
        fin_shape = jax.ShapeDtypeStruct((bsz, 1, nblk * sw), F32)
        fin_spec = pl.BlockSpec((None, 1, sw), lambda g, b: (b, 0, g))
    nrows = tile // L
    wmap = lambda g, b: (g, 0, 0)
    in_specs = [
        pl.BlockSpec((tile, lanes), lambda g, b: (b, g)),
        pl.BlockSpec((None, tw, tw), wmap), pl.BlockSpec((None, tw, sw), wmap),
        pl.BlockSpec((None, sw, tw), wmap), pl.BlockSpec((None, 16, sw), wmap),
    ]
    args = [hn, w1, w2, w3, pw]
    if has_state:
        in_specs.append(pl.BlockSpec((bsz, sw), lambda g, b: (0, g)))
        args.append(s0)
    return pl.pallas_call(
        functools.partial(_s5_core_kernel, nrows=nrows, has_state=has_state),
        grid=(nblk, nb),
        in_specs=in_specs,
        out_specs=[pl.BlockSpec((tile, lanes), lambda g, b: (b, g)), fin_spec],
        out_shape=[jax.ShapeDtypeStruct((m, d), F32), fin_shape],
        compiler_params=_cparams(("parallel", "parallel")),
        name="s5_core",
    )(*args)


def _gelu_tanh(x):
    return 0.5 * x * (1.0 + jnp.tanh(0.7978845608028654 * (x + 0.044715 * (x * x * x))))


def _s5_out_kernel(x_ref, ln_ref, y_ref, d_ref, w_ref, out_ref, *, d):
    x = x_ref[...]
    h = _rms(x, ln_ref[...])
    y = y_ref[...].astype(F32) + d_ref[...] * h
    z = _gelu_tanh(y).astype(BF16)
    ag = jnp.dot(z, w_ref[...], preferred_element_type=F32)
    out_ref[...] = x + ag[:, :d] * _sigmoid(ag[:, d:])


def _s5_out(x, ln, y, d_skip, w_glu, layer_j):
    m, d = x.shape
    tm = min(1024, m)
    return pl.pallas_call(
        functools.partial(_s5_out_kernel, d=d),
        grid=(m // tm,),
        in_specs=[
            pl.BlockSpec((tm, d), lambda i: (i, 0)),
            pl.BlockSpec((1, d), lambda i: (0, 0)),
            pl.BlockSpec((tm, d), lambda i: (i, 0)),
            pl.BlockSpec((1, d), lambda i: (0, 0)),
            pl.BlockSpec((None, d, 2 * d), lambda i: (layer_j, 0, 0)),
        ],
        out_specs=pl.BlockSpec((tm, d), lambda i: (i, 0)),
        out_shape=jax.ShapeDtypeStruct((m, d), F32),
        compiler_params=_cparams(("parallel",)),
        name="s5_out",
    )(x, ln, y, d_skip, w_glu)


def _unpack_state(fin, bsz):
    f = fin.reshape(bsz, -1, 2, S5_GL, S5_P)
    return f[:, :, 0].reshape(bsz, -1, S5_P), f[:, :, 1].reshape(bsz, -1, S5_P)


def _layer(i, st, wts, w_bf):
    j = i // 2
    depth = wts["ln_ffn1"].shape[0]
    x, bsz, seq = st["x"], st["bsz"], st["seq"]
    is_hgrn = i % 2 == 0

    def ffn(name, x, ln, gain2, **kw):
        key = (name, i)
        kw = dict(dict(hn_dtype=None, final_norm=False), **kw)
        if key in w_bf:
            w_in, w_out = w_bf[key]
            out, hn, _ = _ffn_resident(x, ln, w_in, w_out, gain2, **kw)
        else:
            out, hn, w_bf[key] = _ffn_cast(x, ln, wts[name + "_w_in"], wts[name + "_w_out"], i, gain2, **kw)
        return out, hn

    x, hn = ffn("ffn1", x, wts["ln_ffn1"][i:i + 1], wts["ln_mix"][i:i + 1], hn_dtype=BF16 if is_hgrn else F32)
    if is_hgrn:
        cum, masks = _level_consts(min(seq, ROW_BLOCK))
        q, k, v, b, sg = _hgrn_gates(hn, wts["hgrn_w_in"], j, wts["hgrn_lb_logits"], i, cum)
        og, s_t = _hgrn_scan(q, k, v, b, sg, wts["hgrn_gnorm"][j:j + 1], masks, st["hgrn"], j, bsz=bsz, seq=seq)
        st["new_hgrn"].append(s_t)
        x = _hgrn_out(x, og, wts["hgrn_w_out"], j)
    else:
        w1, w2, w3, pw = wts["s5_mats"][j]
        s0 = None
        if st["re"] is not None:
            pack = lambda a: a[j].reshape(bsz, -1, 1, S5_GL, S5_P)
            s0 = jnp.concatenate([pack(st["re"]), pack(st["im"])], axis=2).reshape(bsz, -1)
        y, fin = _s5_core(hn, w1, w2, w3, pw, s0, bsz=bsz, seq=seq)
        f_re, f_im = _unpack_state(fin, bsz)
        st["new_re"].append(f_re)
        st["new_im"].append(f_im)
        x = _s5_out(x, wts["ln_mix"][i:i + 1], y, wts["s5_d"][j:j + 1], wts["s5_w_glu"], j)
    last = i == depth - 1
    gain2 = wts["ln_final"] if last else wts["ln_ffn2"][i:i + 1]
    x, _ = ffn("ffn2", x, wts["ln_ffn2"][i:i + 1], gain2, final_norm=last)
    st["x"] = x


def kernel(x_prompt, x_sample, state_hgrn, state_s5_re, state_s5_im, ln_ffn1, ffn1_w_in, ffn1_w_out, ln_mix, ln_ffn2, ffn2_w_in, ffn2_w_out, hgrn_lb_logits, hgrn_w_in, hgrn_gnorm, hgrn_w_out, s5_a_re, s5_a_im, s5_log_dt, s5_b_re, s5_b_im, s5_c_re, s5_c_im, s5_d, s5_w_glu, ln_final):
    depth, d = ln_ffn1.shape
    wts = dict(
        ln_ffn1=ln_ffn1, ln_mix=ln_mix, ln_ffn2=ln_ffn2, ln_final=ln_final.reshape(1, -1),
        ffn1_w_in=ffn1_w_in, ffn1_w_out=ffn1_w_out, ffn2_w_in=ffn2_w_in, ffn2_w_out=ffn2_w_out,
        hgrn_lb_logits=hgrn_lb_logits, hgrn_w_in=hgrn_w_in.astype(BF16), hgrn_gnorm=hgrn_gnorm,
        hgrn_w_out=hgrn_w_out.astype(BF16), s5_d=s5_d, s5_w_glu=s5_w_glu.astype(BF16),
        s5_mats=[_s5_prep(s5_a_re[j], s5_a_im[j], s5_log_dt[j], s5_b_re[j], s5_b_im[j], s5_c_re[j], s5_c_im[j])
                 for j in range(s5_a_re.shape[0])],
    )

    def stream(x3, hgrn, re, im):
        bsz, seq, _ = x3.shape
        return dict(x=x3.reshape(bsz * seq, d), bsz=bsz, seq=seq, hgrn=hgrn, re=re, im=im,
                    new_hgrn=[], new_re=[], new_im=[])

    sample = stream(x_sample, state_hgrn, state_s5_re, state_s5_im)
    prompt = stream(x_prompt, None, None, None)
    w_bf = {}
    for i in range(depth):
        _layer(i, sample, wts, w_bf)
        _layer(i, prompt, wts, w_bf)

    def outs(st, x3):
        return (st["x"].reshape(x3.shape), jnp.stack(st["new_hgrn"]), jnp.stack(st["new_re"]), jnp.stack(st["new_im"]))

    y_p, hg_p, re_p, im_p = outs(prompt, x_prompt)
    y_s, hg_s, re_s, im_s = outs(sample, x_sample)
    return (y_p, y_s, hg_p, re_p, im_p, hg_s, re_s, im_s)
```

```python
import functools

import numpy as np
import jax
import jax.numpy as jnp
from jax import lax
from jax.experimental import pallas as pl
from jax.experimental.pallas import tpu as pltpu

F32 = jnp.float32
BF16 = jnp.bfloat16

EPS = 1e-6
GATE_FLOOR = 1e-30
MACARON_SCALE = 0.5
LB_CLIP = 1.0 - 1e-4

HEAD_DIM = 128
S5_GC = 16
S5_P = 64
S5_L = 8
S5_GL = 8
ROW_BLOCK = 128
FFN_TM = 1024
FFN_TF = 256
HGRN_TT = 1024
VMEM_LIMIT = 52 * 1024 * 1024


def _cparams(sem):
    return pltpu.CompilerParams(dimension_semantics=sem, vmem_limit_bytes=VMEM_LIMIT)


def _rms(x, gain):
    return x * lax.rsqrt(jnp.mean(x * x, axis=-1, keepdims=True) + EPS) * gain


def _sigmoid(x):
    return 1.0 / (1.0 + jnp.exp(-x))


def _ffn_cast_kernel(x_ref, ln_ref, wa_ref, wb_ref, wout_ref, g2_ref, *rest, nf, hn_dtype, final_norm):
    rest = list(rest)
    out_ref = rest.pop(0)
    hn_ref = rest.pop(0) if hn_dtype is not None else None
    wa_bf_ref, wb_bf_ref, wout_bf_ref, h_scr, acc_scr = rest
    j = pl.program_id(1)

    @pl.when(j == 0)
    def _():
        h_scr[...] = _rms(x_ref[...], ln_ref[...]).astype(BF16)
        acc_scr[...] = jnp.zeros_like(acc_scr)

    wa, wb, wo = wa_ref[...].astype(BF16), wb_ref[...].astype(BF16), wout_ref[...].astype(BF16)
    wa_bf_ref[...] = wa
    wb_bf_ref[...] = wb
    wout_bf_ref[...] = wo
    h = h_scr[...]
    a = jnp.dot(h, wa, preferred_element_type=F32)
    b = jnp.dot(h, wb, preferred_element_type=F32)
    g = (a * _sigmoid(a) * b).astype(BF16)
    acc_scr[...] += jnp.dot(g, wo, preferred_element_type=F32)

    @pl.when(j == nf - 1)
    def _():
        y = x_ref[...] + MACARON_SCALE * acc_scr[...]
        if final_norm:
            out_ref[...] = _rms(y, g2_ref[...])
        else:
            out_ref[...] = y
        if hn_ref is not None:
            hn_ref[...] = _rms(y, g2_ref[...]).astype(hn_ref.dtype)


def _ffn_resident_kernel(x_ref, ln_ref, wa_ref, wb_ref, wout_ref, g2_ref, *rest, tf, hn_dtype, final_norm):
    rest = list(rest)
    out_ref = rest.pop(0)
    hn_ref = rest.pop(0) if hn_dtype is not None else None
    g_scr, = rest
    x = x_ref[...]
    h = _rms(x, ln_ref[...]).astype(BF16)
    for j in range(g_scr.shape[1] // tf):
        cols = slice(j * tf, (j + 1) * tf)
        a = jnp.dot(h, wa_ref[:, cols], preferred_element_type=F32)
        b = jnp.dot(h, wb_ref[:, cols], preferred_element_type=F32)
        g_scr[:, cols] = (a * _sigmoid(a) * b).astype(BF16)
    y = x + MACARON_SCALE * jnp.dot(g_scr[...], wout_ref[...], preferred_element_type=F32)
    if final_norm:
        out_ref[...] = _rms(y, g2_ref[...])
    else:
        out_ref[...] = y
    if hn_ref is not None:
        hn_ref[...] = _rms(y, g2_ref[...]).astype(hn_ref.dtype)


def _ffn_resident(x, ln, w_in, w_out, gain2, *, hn_dtype, final_norm):
    m, d = x.shape
    wa, wb = w_in
    f = w_out.shape[1]
    tm = min(FFN_TM, m)
    row = lambda i: (i, 0)
    once = pl.Buffered(1)
    out_shape = [jax.ShapeDtypeStruct((m, d), F32)]
    out_specs = [pl.BlockSpec((tm, d), row)]
    if hn_dtype is not None:
        out_shape.append(jax.ShapeDtypeStruct((m, d), hn_dtype))
        out_specs.append(pl.BlockSpec((tm, d), row))
    res = pl.pallas_call(
        functools.partial(_ffn_resident_kernel, tf=FFN_TF, hn_dtype=hn_dtype, final_norm=final_norm),
        grid=(m // tm,),
        in_specs=[
            pl.BlockSpec((tm, d), row),
            pl.BlockSpec((1, d), lambda i: (0, 0)),
            pl.BlockSpec((None, d, f), lambda i: (0, 0, 0), pipeline_mode=once),
            pl.BlockSpec((None, d, f), lambda i: (0, 0, 0), pipeline_mode=once),
            pl.BlockSpec((None, f, d), lambda i: (0, 0, 0), pipeline_mode=once),
            pl.BlockSpec((1, d), lambda i: (0, 0)),
        ],
        out_specs=out_specs,
        out_shape=out_shape,
        scratch_shapes=[pltpu.VMEM((tm, f), BF16)],
        compiler_params=_cparams(("parallel",)),
        name="ffn",
    )(x, ln, wa, wb, w_out, gain2)
    return res[0], (res[1] if hn_dtype is not None else None), None


def _ffn_cast(x, ln, w_in, w_out, layer, gain2, *, hn_dtype, final_norm):
    m, d = x.shape
    f = w_out.shape[1]
    tf = FFN_TF
    nf = f // tf
    assert m <= FFN_TM
    row = lambda i, j: (0, 0)
    out_shape = [jax.ShapeDtypeStruct((m, d), F32)]
    out_specs = [pl.BlockSpec((m, d), row)]
    if hn_dtype is not None:
        out_shape.append(jax.ShapeDtypeStruct((m, d), hn_dtype))
        out_specs.append(pl.BlockSpec((m, d), row))
    out_shape += [jax.ShapeDtypeStruct((1, d, f), BF16)] * 2 + [jax.ShapeDtypeStruct((1, f, d), BF16)]
    out_specs += [pl.BlockSpec((None, d, tf), lambda i, j: (0, 0, j))] * 2
    out_specs += [pl.BlockSpec((None, tf, d), lambda i, j: (0, j, 0))]
    res = pl.pallas_call(
        functools.partial(_ffn_cast_kernel, nf=nf, hn_dtype=hn_dtype, final_norm=final_norm),
        grid=(1, nf),
        in_specs=[
            pl.BlockSpec((m, d), row),
            pl.BlockSpec((1, d), row),
            pl.BlockSpec((None, d, tf), lambda i, j: (layer, 0, j)),
            pl.BlockSpec((None, d, tf), lambda i, j: (layer, 0, nf + j)),
            pl.BlockSpec((None, tf, d), lambda i, j: (layer, j, 0)),
            pl.BlockSpec((1, d), row),
        ],
        out_specs=out_specs,
        out_shape=out_shape,
        scratch_shapes=[pltpu.VMEM((m, d), BF16), pltpu.VMEM((m, d), F32)],
        compiler_params=_cparams(("arbitrary", "arbitrary")),
        name="ffn_cast",
    )(x, ln, w_in, w_in, w_out, gain2)
    res = list(res)
    out = res.pop(0)
    hn = res.pop(0) if hn_dtype is not None else None
    return out, hn, ((res[0], res[1]), res[2])


def _hgrn_gates_kernel(hn_ref, w_ref, lbl_ref, cum_ref, q_ref, k_ref, v_ref, b_ref, sg_ref, *, layer, d, nh):
    h = hn_ref[...]
    logits = lbl_ref[...]
    e = jnp.exp(logits - jnp.max(logits, axis=0, keepdims=True))
    p = e / jnp.sum(e, axis=0, keepdims=True)
    cum = p[0:1]
    for r in range(1, layer + 1):
        cum = cum + p[r:r + 1]
    lb = jnp.clip(cum - p[0:1], 0.0, LB_CLIP)

    def heads(ref, val):
        for hh in range(nh):
            ref[hh] = val[:, hh * HEAD_DIM:(hh + 1) * HEAD_DIM].astype(ref.dtype)

    pq = jnp.dot(h, w_ref[:, 0:d], preferred_element_type=F32)
    heads(q_ref, pq)
    pf = jnp.dot(h, w_ref[:, d:2 * d], preferred_element_type=F32)
    ez = jnp.exp(-jnp.abs(pf))
    r = 1.0 / (1.0 + ez)
    pos = pf >= 0.0
    sig_p = jnp.where(pos, r, ez * r)
    sig_n = jnp.where(pos, ez * r, r)
    fg = lb + (1.0 - lb) * sig_p
    hi, mid, lo = _split3(jnp.log(jnp.maximum(fg, GATE_FLOOR)))
    cum = cum_ref[...]
    dot = functools.partial(jnp.dot, preferred_element_type=F32)
    R = ROW_BLOCK
    for rb in range(h.shape[0] // R):
        rows = slice(rb * R, (rb + 1) * R)
        b = (dot(cum, hi[rows]) + dot(cum, mid[rows])) + dot(cum, lo[rows])
        for hh in range(nh):
            b_ref[hh, rows, :] = b[:, hh * HEAD_DIM:(hh + 1) * HEAD_DIM]
    heads(k_ref, (1.0 - lb) * sig_n)
    pv = jnp.dot(h, w_ref[:, 2 * d:3 * d], preferred_element_type=F32)
    heads(v_ref, pv)
    pg = jnp.dot(h, w_ref[:, 3 * d:4 * d], preferred_element_type=F32)
    heads(sg_ref, pg * _sigmoid(pg))


def _hgrn_gates(hn, w_in, layer_j, lb_logits, layer, cum):
    m, d = hn.shape
    nh = d // HEAD_DIM
    tm = min(512, m)
    hm = lambda dt: jax.ShapeDtypeStruct((nh, m, HEAD_DIM), dt)
    hspec = pl.BlockSpec((nh, tm, HEAD_DIM), lambda i: (0, i, 0))
    return pl.pallas_call(
        functools.partial(_hgrn_gates_kernel, layer=layer, d=d, nh=nh),
        grid=(m // tm,),
        in_specs=[
            pl.BlockSpec((tm, d), lambda i: (i, 0)),
            pl.BlockSpec((None, d, 4 * d), lambda i: (layer_j, 0, 0)),
            pl.BlockSpec(lb_logits.shape, lambda i: (0, 0)),
            pl.BlockSpec(cum.shape, lambda i: (0, 0)),
        ],
        out_specs=[hspec] * 5,
        out_shape=[hm(BF16), hm(BF16), hm(BF16), hm(F32), hm(BF16)],
        compiler_params=_cparams(("parallel",)),
        name="hgrn_gates",
    )(hn, w_in, lb_logits, cum)


def _anchor(b, m):
    R, n = b.shape
    if m >= 8:
        parts = [jnp.broadcast_to(b[blk * 2 * m + m - 1:blk * 2 * m + m, :], (2 * m, n)) for blk in range(R // (2 * m))]
        return parts[0] if len(parts) == 1 else jnp.concatenate(parts, axis=0)
    if m == 1:
        odd = jnp.bitwise_and(lax.broadcasted_iota(jnp.int32, (R, 1), 0), 1) == 1
        return jnp.where(odd, pltpu.roll(b, 1, 0), b)
    b3 = b.reshape(R // 8, 8, n)
    if m == 4:
        a3 = jnp.broadcast_to(b3[:, 3:4, :], b3.shape)
    else:
        sub = lax.broadcasted_iota(jnp.int32, (1, 8, 1), 1)
        a3 = jnp.where(sub < 4, jnp.broadcast_to(b3[:, 1:2, :], b3.shape), jnp.broadcast_to(b3[:, 5:6, :], b3.shape))
    return a3.reshape(R, n)


def _block_end(b, tc):
    R, n = b.shape
    if tc == 8:
        b3 = b.reshape(R // 8, 8, n)
        return jnp.broadcast_to(b3[:, 7:8, :], b3.shape).reshape(R, n)
    parts = [jnp.broadcast_to(b[blk * tc + tc - 1:blk * tc + tc, :], (tc, n)) for blk in range(R // tc)]
    return parts[0] if len(parts) == 1 else jnp.concatenate(parts, axis=0)


def _hgrn_scan_kernel(*refs, tc, nseq, nrb, nt, has_state):
    if has_state:
        q_ref, k_ref, v_ref, b_ref, sg_ref, gn_ref, msk_ref, s0_ref, o_ref, st_ref, s_scr = refs
    else:
        q_ref, k_ref, v_ref, b_ref, sg_ref, gn_ref, msk_ref, o_ref, st_ref, s_scr = refs
    R = ROW_BLOCK
    t_idx = pl.program_id(2)
    dot = functools.partial(jnp.dot, preferred_element_type=F32)
    dn = (((1,), (1,)), ((), ()))
    dn0 = (((0,), (0,)), ((), ()))

    if nseq == 1:
        @pl.when(t_idx == 0)
        def _():
            if has_state:
                s_scr[...] = s0_ref[...].reshape(s_scr.shape)
            else:
                s_scr[...] = jnp.zeros_like(s_scr)

    seq_of_row = lax.broadcasted_iota(jnp.int32, (R, 1), 0) // tc
    gn = gn_ref[...]

    def local_part(rb):
        sl = pl.ds(rb * R, R)
        q = q_ref[sl, :].astype(F32)
        kk = k_ref[sl, :].astype(F32)
        vb = v_ref[sl, :]
        b = b_ref[sl, :]
        scores = lax.dot_general(q_ref[sl, :], k_ref[sl, :], dn, preferred_element_type=F32) * msk_ref[0]
        m, level = 1, 1
        while m < min(tc, 8):
            w = jnp.exp(-jnp.abs(b - _anchor(b, m)))
            s_m = lax.dot_general((q * w).astype(BF16), (kk * w).astype(BF16), dn, preferred_element_type=F32)
            scores = scores + s_m * msk_ref[level]
            m *= 2
            level += 1
        while m < tc:
            nblk = R // (2 * m)
            q_parts, k_parts, spans = [], [], []
            for blk in range(nblk):
                lo, mid, hi = blk * 2 * m, blk * 2 * m + m, (blk + 1) * 2 * m
                anchor = b[mid - 1:mid, :]
                q_parts.append(q[mid:hi] * jnp.exp(b[mid:hi] - anchor))
                k_parts += [kk[lo:mid] * jnp.exp(anchor - b[lo:mid]), jnp.zeros((m, HEAD_DIM), F32)]
                spans.append((lo, mid, hi))
            s_m = lax.dot_general(jnp.concatenate(q_parts, axis=0).astype(BF16),
                                  jnp.concatenate(k_parts, axis=0).astype(BF16), dn, preferred_element_type=F32)
            rows = []
            for blk, (lo, mid, hi) in enumerate(spans):
                piece = s_m[blk * m:(blk + 1) * m]
                if nblk > 1:
                    piece = piece * msk_ref[level, mid:hi, :]
                rows += [scores[lo:mid], scores[mid:hi] + piece]
            scores = jnp.concatenate(rows, axis=0)
            m *= 2
            level += 1
        e_end = _block_end(b, tc)
        qe = q * jnp.exp(b)
        ke = kk * jnp.exp(e_end - b)
        if nseq > 1:
            qe = jnp.concatenate([jnp.where(seq_of_row == i, qe, 0.0) for i in range(nseq)], axis=1)
            ke = jnp.concatenate([jnp.where(seq_of_row == i, ke, 0.0) for i in range(nseq)], axis=1)
        ds = lax.dot_general(ke.astype(BF16), vb, dn0, preferred_element_type=F32)
        decs = []
        for i in range(nseq):
            dec_row = jnp.exp(e_end[i * tc:i * tc + 1, :])
            decs.append(jnp.transpose(jnp.broadcast_to(dec_row, (HEAD_DIM, HEAD_DIM))))
        dec = decs[0] if nseq == 1 else jnp.concatenate(decs, axis=0)
        return jnp.concatenate([scores.astype(BF16), qe.astype(BF16)], axis=1), vb, ds, dec

    parts = [local_part(rb) for rb in range(nrb)]

    s_run = s_scr[...] if nseq == 1 else None
    for rb in range(nrb):
        sq, vb, ds, dec = parts[rb]
        sl = pl.ds(rb * R, R)
        if nseq == 1:
            s_in = s_run
        else:
            s_in = s0_ref[rb * nseq:(rb + 1) * nseq].reshape(nseq * HEAD_DIM, HEAD_DIM)
        o = dot(sq, jnp.concatenate([vb, s_in.astype(BF16)], axis=0))
        s_new = dec * s_in + ds
        on = o * lax.rsqrt(jnp.mean(o * o, axis=-1, keepdims=True) + EPS) * gn
        o_ref[sl, :] = (on * sg_ref[sl, :].astype(F32)).astype(o_ref.dtype)
        if nseq == 1:
            s_run = s_new
        else:
            st_ref[rb * nseq:(rb + 1) * nseq] = s_new.reshape(nseq, HEAD_DIM, HEAD_DIM)

    if nseq == 1:
        s_scr[...] = s_run

        @pl.when(t_idx == nt - 1)
        def _():
            st_ref[...] = s_run.reshape(st_ref.shape)


def _level_consts(tc):
    R = ROW_BLOCK
    t = np.arange(R)[:, None]
    s = np.arange(R)[None, :]
    cum = ((s <= t) & (t // tc == s // tc)).astype(np.float32)
    x = t ^ s
    masks = [t == s]
    m = 1
    while m < tc:
        masks.append((t > s) & (x >= m) & (x < 2 * m))
        m *= 2
    return jnp.asarray(cum, BF16), jnp.asarray(np.stack(masks), F32)


def _hgrn_scan(q, k, v, b, sg, gnorm, masks, state, layer_j, *, bsz, seq):
    nh, m, _ = q.shape
    R = ROW_BLOCK
    has_state = state is not None
    if seq >= R:
        tc, nseq = R, 1
        tt = min(HGRN_TT, seq)
        nt = seq // tt
        nb = 1
        grid = (nh, bsz, nt)
        row_map = lambda h, b, t: (h, b * nt + t, 0)
    else:
        assert has_state
        tc, nseq = seq, R // seq
        tt = min(HGRN_TT, m)
        nt = 1
        nb = tt // seq
        grid = (nh, m // tt, 1)
        row_map = lambda h, b, t: (h, b, 0)
    rspec = pl.BlockSpec((None, tt, HEAD_DIM), row_map)
    const2 = lambda h, b, t: (0, 0)
    in_specs = [rspec] * 5 + [pl.BlockSpec((1, HEAD_DIM), const2), pl.BlockSpec(masks.shape, lambda h, b, t: (0, 0, 0))]
    args = [q, k, v, b, sg, gnorm, masks]
    if has_state:
        in_specs.append(pl.BlockSpec((None, nb, None, HEAD_DIM, HEAD_DIM), lambda h, b, t: (layer_j, b, h, 0, 0)))
        args.append(state)
    return pl.pallas_call(
        functools.partial(_hgrn_scan_kernel, tc=tc, nseq=nseq, nrb=tt // R, nt=nt, has_state=has_state),
        grid=grid,
        in_specs=in_specs,
        out_specs=[rspec, pl.BlockSpec((nb, None, HEAD_DIM, HEAD_DIM), lambda h, b, t: (b, h, 0, 0))],
        out_shape=[jax.ShapeDtypeStruct((nh, m, HEAD_DIM), BF16),
                   jax.ShapeDtypeStruct((bsz, nh, HEAD_DIM, HEAD_DIM), F32)],
        scratch_shapes=[pltpu.VMEM((HEAD_DIM, HEAD_DIM), F32)],
        compiler_params=_cparams(("parallel", "parallel", "arbitrary")),
        name="hgrn_scan",
    )(*args)


def _hgrn_out_kernel(x_ref, og_ref, w_ref, out_ref, *, nh):
    og = jnp.concatenate([og_ref[hh] for hh in range(nh)], axis=-1)
    out_ref[...] = x_ref[...] + jnp.dot(og, w_ref[...], preferred_element_type=F32)


def _hgrn_out(x, og, w_out, layer_j):
    m, d = x.shape
    nh = og.shape[0]
    tm = min(1024, m)
    return pl.pallas_call(
        functools.partial(_hgrn_out_kernel, nh=nh),
        grid=(m // tm,),
        in_specs=[
            pl.BlockSpec((tm, d), lambda i: (i, 0)),
            pl.BlockSpec((nh, tm, HEAD_DIM), lambda i: (0, i, 0)),
            pl.BlockSpec((None, d, d), lambda i: (layer_j, 0, 0)),
        ],
        out_specs=pl.BlockSpec((tm, d), lambda i: (i, 0)),
        out_shape=jax.ShapeDtypeStruct((m, d), F32),
        compiler_params=_cparams(("parallel",)),
        name="hgrn_out",
    )(x, og, w_out)


def _split3(a):
    hi = a.astype(BF16)
    r1 = a - hi.astype(F32)
    mid = r1.astype(BF16)
    lo = (r1 - mid.astype(F32)).astype(BF16)
    return hi, mid, lo


def _dot_precise(a, b):
    ah, am, al = _split3(a)
    bh, bm, bl = _split3(b)
    d = functools.partial(jnp.dot, preferred_element_type=F32)
    return (d(ah, bh) + (d(ah, bm) + d(am, bh))) + ((d(am, bm) + d(ah, bl)) + d(al, bh))


def _s5_prep_kernel(arc_ref, aic_ref, arr_ref, air_ref, ldt_ref, x1_ref, x2_ref, y1_ref, y2_ref, plt_ref, pls_ref,
                    w1_ref, w2_ref, w3_ref, pw_ref):
    L, GC, P, GL = S5_L, S5_GC, S5_P, S5_GL
    W = L * GC
    TW = L * GL * GC
    lane_t = lax.broadcasted_iota(jnp.int32, (1, W), 1) // GC
    row_s = lax.broadcasted_iota(jnp.int32, (W, 1), 0) // GC
    sgn_l = jnp.where(lax.broadcasted_iota(jnp.int32, (1, 2 * P), 1) < P, -1.0, 1.0)
    sgn_s = jnp.where(lax.broadcasted_iota(jnp.int32, (2 * P, 1), 0) < P, 1.0, -1.0)
    row16 = lax.broadcasted_iota(jnp.int32, (16, 1), 0)
    dot = functools.partial(jnp.dot, preferred_element_type=F32)

    def cmul(ar, ai, br, bi):
        return ar * br - ai * bi, ar * bi + ai * br

    bases = []
    pw = jnp.zeros((16, GL * 2 * P), F32)
    for g in range(GL):
        dt = jnp.exp(ldt_ref[g])
        a_re_c, a_im_c = arc_ref[g], aic_ref[g]
        er = jnp.exp(a_re_c * dt)
        lbr_c, lbi_c = er * jnp.cos(a_im_c * dt), er * jnp.sin(a_im_c * dt)
        a_re_r, a_im_r = arr_ref[g], air_ref[g]
        er_r = jnp.exp(a_re_r * dt)
        lbr_r, lbi_r = er_r * jnp.cos(a_im_r * dt), er_r * jnp.sin(a_im_r * dt)
        den = a_re_r * a_re_r + a_im_r * a_im_r
        nr, ni = lbr_r - 1.0, lbi_r
        f_r = (nr * a_re_r + ni * a_im_r) / den
        f_i = (ni * a_re_r - nr * a_im_r) / den

        pc = [(jnp.ones_like(lbr_c), jnp.zeros_like(lbr_c))]
        pr = [(jnp.ones_like(lbr_r), jnp.zeros_like(lbr_r))]
        for _ in range(L):
            pc.append(cmul(pc[-1][0], pc[-1][1], lbr_c, lbi_c))
            pr.append(cmul(pr[-1][0], pr[-1][1], lbr_r, lbi_r))

        q_re = jnp.zeros((2 * P, W), F32)
        q_im = jnp.zeros((2 * P, W), F32)
        for t in range(L):
            sel = lane_t == t
            q_re = jnp.where(sel, pc[t][0], q_re)
            q_im = jnp.where(sel, pc[t][1], q_im)
        y1, y2 = y1_ref[g], y2_ref[g]
        mp = sgn_s * (q_re * y1) - q_im * y2
        q1_re, q1_im = cmul(q_re, q_im, lbr_c, lbi_c)
        w3g = (sgn_s * (q1_re * y1) - q1_im * y2).astype(BF16)
        z3 = dot(w3g, plt_ref[g]).astype(BF16)
        for ri in range(2):
            w3_ref[ri * GL * P + g * P:ri * GL * P + (g + 1) * P, :] = z3[ri * P:(ri + 1) * P]

        cf_re = jnp.zeros((W, 2 * P), F32)
        cf_im = jnp.zeros((W, 2 * P), F32)
        for s in range(L):
            sel = row_s == s
            cf_re = jnp.where(sel, pr[L - 1 - s][0], cf_re)
            cf_im = jnp.where(sel, pr[L - 1 - s][1], cf_im)
        cf_re, cf_im = cmul(cf_re, cf_im, f_r, f_i)
        x1, x2 = x1_ref[g], x2_ref[g]
        w2g = (cf_re * x1 + (cf_im * sgn_l) * x2).astype(BF16)
        v2 = dot(w2g, pls_ref[g]).astype(BF16)
        for s in range(L):
            w2_ref[s * GL * GC + g * GC:s * GL * GC + (g + 1) * GC, :] = v2[s * GC:(s + 1) * GC]

        bbp = f_r * x1[0:GC] + (f_i * sgn_l) * x2[0:GC]
        base = _dot_precise(bbp, mp).astype(BF16)
        bases.append(dot(base, plt_ref[g]))

        tab = jnp.zeros((16, 2 * P), F32)
        cur = pr[L]
        for k in range(8):
            tab = jnp.where(row16 == k, cur[0], tab)
            tab = jnp.where(row16 == 8 + k, cur[1] * sgn_l, tab)
            cur = cmul(cur[0], cur[1], cur[0], cur[1])
        th, tm_, tl = _split3(tab)
        pls = pls_ref[g]
        pw = pw + ((dot(th, pls) + dot(tm_, pls)) + dot(tl, pls))

    base8 = jnp.concatenate(bases, axis=0)
    lane = lax.broadcasted_iota(jnp.int32, (1, TW), 1)
    blk = GL * GC
    w1_ref[0:blk, :] = base8.astype(BF16)
    for s in range(1, L):
        w1_ref[s * blk:(s + 1) * blk, :] = jnp.where(lane >= s * blk, pltpu.roll(base8, s * blk, 1), 0.0).astype(BF16)
    pw_ref[...] = pw


def _s5_placements():
    L, GC, P, GL = S5_L, S5_GC, S5_P, S5_GL
    plt = np.zeros((GL, L * GC, L * GL * GC), np.float32)
    pls = np.zeros((GL, 2 * P, 2 * GL * P), np.float32)
    for gl in range(GL):
        for t in range(L):
            for c in range(GC):
                plt[gl, t * GC + c, t * GL * GC + gl * GC + c] = 1.0
        for ri in range(2):
            for p in range(P):
                pls[gl, ri * P + p, ri * GL * P + gl * P + p] = 1.0
    return jnp.asarray(plt, BF16), jnp.asarray(pls, BF16)


def _s5_prep(a_re, a_im, log_dt, b_re, b_im, c_re, c_im):
    g, p = a_re.shape
    L, GC, GL = S5_L, S5_GC, S5_GL
    W = L * GC
    TW = L * GL * GC
    SW = 2 * GL * p
    nblk = g // GL
    a_re2 = jnp.concatenate([a_re, a_re], axis=1)
    a_im2 = jnp.concatenate([a_im, a_im], axis=1)
    arc, aic = a_re2[:, :, None], a_im2[:, :, None]
    arr, air = a_re2[:, None, :], a_im2[:, None, :]
    ldt = log_dt[:, None, None]
    btr, bti = jnp.transpose(b_re, (0, 2, 1)), jnp.transpose(b_im, (0, 2, 1))
    x1 = jnp.tile(jnp.concatenate([btr, bti], axis=2), (1, L, 1))
    x2 = jnp.tile(jnp.concatenate([bti, btr], axis=2), (1, L, 1))
    ctr, cti = jnp.transpose(c_re, (0, 2, 1)), jnp.transpose(c_im, (0, 2, 1))
    y1 = jnp.tile(jnp.concatenate([ctr, cti], axis=1), (1, 1, L))
    y2 = jnp.tile(jnp.concatenate([cti, ctr], axis=1), (1, 1, L))
    plt, pls = _s5_placements()

    def spec(shape):
        return pl.BlockSpec((GL,) + shape, lambda i: (i, 0, 0))

    const3 = lambda i: (0, 0, 0)
    out3 = lambda i: (i, 0, 0)
    return pl.pallas_call(
        _s5_prep_kernel,
        grid=(nblk,),
        in_specs=[spec((2 * p, 1)), spec((2 * p, 1)), spec((1, 2 * p)), spec((1, 2 * p)), spec((1, 1)),
                  spec((W, 2 * p)), spec((W, 2 * p)), spec((2 * p, W)), spec((2 * p, W)),
                  pl.BlockSpec(plt.shape, const3), pl.BlockSpec(pls.shape, const3)],
        out_specs=[pl.BlockSpec((None, TW, TW), out3), pl.BlockSpec((None, TW, SW), out3),
                   pl.BlockSpec((None, SW, TW), out3), pl.BlockSpec((None, 16, SW), out3)],
        out_shape=[jax.ShapeDtypeStruct((nblk, TW, TW), BF16), jax.ShapeDtypeStruct((nblk, TW, SW), BF16),
                   jax.ShapeDtypeStruct((nblk, SW, TW), BF16), jax.ShapeDtypeStruct((nblk, 16, SW), F32)],
        compiler_params=_cparams(("parallel",)),
        name="s5_prep",
    )(arc, aic, arr, air, ldt, x1, x2, y1, y2, plt, pls)


def _s5_core_kernel(*refs, nrows, has_state):
    if has_state:
        hn_ref, w1_ref, w2_ref, w3_ref, pw_ref, s0_ref, y_ref, fin_ref = refs
    else:
        hn_ref, w1_ref, w2_ref, w3_ref, pw_ref, y_ref, fin_ref = refs
    L = S5_L
    half = S5_GL * S5_P
    dot = functools.partial(jnp.dot, preferred_element_type=F32)

    u = jnp.concatenate([hn_ref[pl.ds(s, nrows, stride=L), :] for s in range(L)], axis=1).astype(BF16)
    x = dot(u, w2_ref[...])
    if has_state:
        s_in = s0_ref[...]
        fin_ref[...] = x + (s_in * pw_ref[0:1, :] + pltpu.roll(s_in, half, 1) * pw_ref[8:9, :])
    else:
        row = lax.broadcasted_iota(jnp.int32, (nrows, 1), 0)

        def shift(a, sh):
            if sh % 8 == 0:
                return jnp.concatenate([jnp.zeros((sh, a.shape[1]), F32), a[:nrows - sh]], axis=0)
            return jnp.where(row >= sh, pltpu.roll(a, sh, 0), 0.0)

        re_strips, im_strips = [], []
        for st in range(half // 128):
            lr = slice(st * 128, (st + 1) * 128)
            li = slice(half + st * 128, half + (st + 1) * 128)
            xr, xi = x[:, lr], x[:, li]
            sh, k = 1, 0
            while sh < nrows:
                wr, wi = pw_ref[k:k + 1, lr], pw_ref[8 + k:9 + k, li]
                sr, si = shift(xr, sh), shift(xi, sh)
                xr, xi = xr + (sr * wr - si * wi), xi + (sr * wi + si * wr)
                sh *= 2
                k += 1
            re_strips.append(xr)
            im_strips.append(xi)
        x = jnp.concatenate(re_strips + im_strips, axis=1)
        s_in = shift(x, 1)
        fin_ref[...] = x[nrows - 1:nrows, :]
    y = dot(u, w1_ref[...]) + dot(s_in.astype(BF16), w3_ref[...])
    lanes = y_ref.shape[1]
    for t in range(L):
        y_ref[pl.ds(t, nrows, stride=L), :] = y[:, t * lanes:(t + 1) * lanes]


def _s5_core(hn, w1, w2, w3, pw, s0, *, bsz, seq):
    m, d = hn.shape
    L = S5_L
    nblk, tw, sw = w2.shape
    lanes = d // nblk
    has_state = s0 is not None
    assert has_state == (seq == L) and seq % L == 0 and seq // L <= 256
    if has_state:
        tile, nb = m, 1
        fin_shape = jax.ShapeDtypeStruct((bsz, nblk * sw), F32)
        fin_spec = pl.BlockSpec((bsz, sw), lambda g, b: (0, g))
    else:
        tile, nb = seq, bsz
        fin_shape = jax.ShapeDtypeStruct((bsz, 1, nblk * sw), F32)
        fin_spec = pl.BlockSpec((None, 1, sw), lambda g, b: (b, 0, g))
    nrows = tile // L
    wmap = lambda g, b: (g, 0, 0)
    in_specs = [
        pl.BlockSpec((tile, lanes), lambda g, b: (b, g)),
        pl.BlockSpec((None, tw, tw), wmap), pl.BlockSpec((None, tw, sw), wmap),
        pl.BlockSpec((None, sw, tw), wmap), pl.BlockSpec((None, 16, sw), wmap),
    ]
    args = [hn, w1, w2, w3, pw]
    if has_state:
        in_specs.append(pl.BlockSpec((bsz, sw), lambda g, b: (0, g)))
        args.append(s0)
    return pl.pallas_call(
        functools.partial(_s5_core_kernel, nrows=nrows, has_state=has_state),
        grid=(nblk, nb),
        in_specs=in_specs,
        out_specs=[pl.BlockSpec((tile, lanes), lambda g, b: (b, g)), fin_spec],
        out_shape=[jax.ShapeDtypeStruct((m, d), F32), fin_shape],
        compiler_params=_cparams(("parallel", "parallel")),
        name="s5_core",
    )(*args)


def _gelu_tanh(x):
    return 0.5 * x * (1.0 + jnp.tanh(0.7978845608028654 * (x + 0.044715 * (x * x * x))))


def _s5_out_kernel(x_ref, ln_ref, y_ref, d_ref, w_ref, out_ref, *, d):
    x = x_ref[...]
    h = _rms(x, ln_ref[...])
    y = y_ref[...].astype(F32) + d_ref[...] * h
    z = _gelu_tanh(y).astype(BF16)
    ag = jnp.dot(z, w_ref[...], preferred_element_type=F32)
    out_ref[...] = x + ag[:, :d] * _sigmoid(ag[:, d:])


def _s5_out(x, ln, y, d_skip, w_glu, layer_j):
    m, d = x.shape
    tm = min(1024, m)
    return pl.pallas_call(
        functools.partial(_s5_out_kernel, d=d),
        grid=(m // tm,),
        in_specs=[
            pl.BlockSpec((tm, d), lambda i: (i, 0)),
            pl.BlockSpec((1, d), lambda i: (0, 0)),
            pl.BlockSpec((tm, d), lambda i: (i, 0)),
            pl.BlockSpec((1, d), lambda i: (0, 0)),
            pl.BlockSpec((None, d, 2 * d), lambda i: (layer_j, 0, 0)),
        ],
        out_specs=pl.BlockSpec((tm, d), lambda i: (i, 0)),
        out_shape=jax.ShapeDtypeStruct((m, d), F32),
        compiler_params=_cparams(("parallel",)),
        name="s5_out",
    )(x, ln, y, d_skip, w_glu)


def _unpack_state(fin, bsz):
    f = fin.reshape(bsz, -1, 2, S5_GL, S5_P)
    return f[:, :, 0].reshape(bsz, -1, S5_P), f[:, :, 1].reshape(bsz, -1, S5_P)


def _layer(i, st, wts, w_bf):
    j = i // 2
    depth = wts["ln_ffn1"].shape[0]
    x, bsz, seq = st["x"], st["bsz"], st["seq"]
    is_hgrn = i % 2 == 0

    def ffn(name, x, ln, gain2, **kw):
        key = (name, i)
        kw = dict(dict(hn_dtype=None, final_norm=False), **kw)
        if key in w_bf:
            w_in, w_out = w_bf[key]
            out, hn, _ = _ffn_resident(x, ln, w_in, w_out, gain2, **kw)
        else:
            out, hn, w_bf[key] = _ffn_cast(x, ln, wts[name + "_w_in"], wts[name + "_w_out"], i, gain2, **kw)
        return out, hn

    x, hn = ffn("ffn1", x, wts["ln_ffn1"][i:i + 1], wts["ln_mix"][i:i + 1], hn_dtype=BF16 if is_hgrn else F32)
    if is_hgrn:
        cum, masks = _level_consts(min(seq, ROW_BLOCK))
        q, k, v, b, sg = _hgrn_gates(hn, wts["hgrn_w_in"], j, wts["hgrn_lb_logits"], i, cum)
        og, s_t = _hgrn_scan(q, k, v, b, sg, wts["hgrn_gnorm"][j:j + 1], masks, st["hgrn"], j, bsz=bsz, seq=seq)
        st["new_hgrn"].append(s_t)
        x = _hgrn_out(x, og, wts["hgrn_w_out"], j)
    else:
        w1, w2, w3, pw = wts["s5_mats"][j]
        s0 = None
        if st["re"] is not None:
            pack = lambda a: a[j].reshape(bsz, -1, 1, S5_GL, S5_P)
            s0 = jnp.concatenate([pack(st["re"]), pack(st["im"])], axis=2).reshape(bsz, -1)
        y, fin = _s5_core(hn, w1, w2, w3, pw, s0, bsz=bsz, seq=seq)
        f_re, f_im = _unpack_state(fin, bsz)
        st["new_re"].append(f_re)
        st["new_im"].append(f_im)
        x = _s5_out(x, wts["ln_mix"][i:i + 1], y, wts["s5_d"][j:j + 1], wts["s5_w_glu"], j)
    last = i == depth - 1
    gain2 = wts["ln_final"] if last else wts["ln_ffn2"][i:i + 1]
    x, _ = ffn("ffn2", x, wts["ln_ffn2"][i:i + 1], gain2, final_norm=last)
    st["x"] = x


def kernel(x_prompt, x_sample, state_hgrn, state_s5_re, state_s5_im, ln_ffn1, ffn1_w_in, ffn1_w_out, ln_mix, ln_ffn2, ffn2_w_in, ffn2_w_out, hgrn_lb_logits, hgrn_w_in, hgrn_gnorm, hgrn_w_out, s5_a_re, s5_a_im, s5_log_dt, s5_b_re, s5_b_im, s5_c_re, s5_c_im, s5_d, s5_w_glu, ln_final):
    depth, d = ln_ffn1.shape
    wts = dict(
        ln_ffn1=ln_ffn1, ln_mix=ln_mix, ln_ffn2=ln_ffn2, ln_final=ln_final.reshape(1, -1),
        ffn1_w_in=ffn1_w_in, ffn1_w_out=ffn1_w_out, ffn2_w_in=ffn2_w_in, ffn2_w_out=ffn2_w_out,
        hgrn_lb_logits=hgrn_lb_logits, hgrn_w_in=hgrn_w_in.astype(BF16), hgrn_gnorm=hgrn_gnorm,
        hgrn_w_out=hgrn_w_out.astype(BF16), s5_d=s5_d, s5_w_glu=s5_w_glu.astype(BF16),
        s5_mats=[_s5_prep(s5_a_re[j], s5_a_im[j], s5_log_dt[j], s5_b_re[j], s5_b_im[j], s5_c_re[j], s5_c_im[j])
                 for j in range(s5_a_re.shape[0])],
    )

    def stream(x3, hgrn, re, im):
        bsz, seq, _ = x3.shape
        return dict(x=x3.reshape(bsz * seq, d), bsz=bsz, seq=seq, hgrn=hgrn, re=re, im=im,
                    new_hgrn=[], new_re=[], new_im=[])

    sample = stream(x_sample, state_hgrn, state_s5_re, state_s5_im)
    prompt = stream(x_prompt, None, None, None)
    w_bf = {}
    for i in range(depth):
        _layer(i, sample, wts, w_bf)
        _layer(i, prompt, wts, w_bf)

    def outs(st, x3):
        return (st["x"].reshape(x3.shape), jnp.stack(st["new_hgrn"]), jnp.stack(st["new_re"]), jnp.stack(st["new_im"]))

    y_p, hg_p, re_p, im_p = outs(prompt, x_prompt)
    y_s, hg_s, re_s, im_s = outs(sample, x_sample)
    return (y_p, y_s, hg_p, re_p, im_p, hg_s, re_s, im_s)
```

```python
import functools

import numpy as np
import jax
import jax.numpy as jnp
from jax import lax
from jax.experimental import pallas as pl
from jax.experimental.pallas import tpu as pltpu

F32 = jnp.float32
BF16 = jnp.bfloat16

EPS = 1e-6
GATE_FLOOR = 1e-30
MACARON_SCALE = 0.5
LB_CLIP = 1.0 - 1e-4

HEAD_DIM = 128
S5_GC = 16
S5_P = 64
S5_L = 8
S5_GL = 8
ROW_BLOCK = 128
FFN_TM = 1024
FFN_TF = 256
HGRN_TT = 1024
HGRN_TT_SHORT = 512
VMEM_LIMIT = 52 * 1024 * 1024


def _cparams(sem):
    return pltpu.CompilerParams(dimension_semantics=sem, vmem_limit_bytes=VMEM_LIMIT)


def _rms(x, gain):
    return x * lax.rsqrt(jnp.mean(x * x, axis=-1, keepdims=True) + EPS) * gain


def _sigmoid(x):
    return 1.0 / (1.0 + jnp.exp(-x))


def _ffn_cast_kernel(x_ref, ln_ref, wa_ref, wb_ref, wout_ref, g2_ref, *rest, nf, hn_dtype, final_norm):
    rest = list(rest)
    out_ref = rest.pop(0)
    hn_ref = rest.pop(0) if hn_dtype is not None else None
    wa_bf_ref, wb_bf_ref, wout_bf_ref, h_scr, acc_scr = rest
    j = pl.program_id(1)

    @pl.when(j == 0)
    def _():
        h_scr[...] = _rms(x_ref[...], ln_ref[...]).astype(BF16)
        acc_scr[...] = jnp.zeros_like(acc_scr)

    wa, wb, wo = wa_ref[...].astype(BF16), wb_ref[...].astype(BF16), wout_ref[...].astype(BF16)
    wa_bf_ref[...] = wa
    wb_bf_ref[...] = wb
    wout_bf_ref[...] = wo
    h = h_scr[...]
    a = jnp.dot(h, wa, preferred_element_type=F32)
    b = jnp.dot(h, wb, preferred_element_type=F32)
    g = (a * _sigmoid(a) * b).astype(BF16)
    acc_scr[...] += jnp.dot(g, wo, preferred_element_type=F32)

    @pl.when(j == nf - 1)
    def _():
        y = x_ref[...] + MACARON_SCALE * acc_scr[...]
        if final_norm:
            out_ref[...] = _rms(y, g2_ref[...])
        else:
            out_ref[...] = y
        if hn_ref is not None:
            hn_ref[...] = _rms(y, g2_ref[...]).astype(hn_ref.dtype)


def _ffn_resident_kernel(x_ref, ln_ref, wa_ref, wb_ref, wout_ref, g2_ref, *rest, tf, hn_dtype, final_norm):
    rest = list(rest)
    out_ref = rest.pop(0)
    hn_ref = rest.pop(0) if hn_dtype is not None else None
    g_scr, = rest
    x = x_ref[...]
    h = _rms(x, ln_ref[...]).astype(BF16)
    for j in range(g_scr.shape[1] // tf):
        cols = slice(j * tf, (j + 1) * tf)
        a = jnp.dot(h, wa_ref[:, cols], preferred_element_type=F32)
        b = jnp.dot(h, wb_ref[:, cols], preferred_element_type=F32)
        g_scr[:, cols] = (a * _sigmoid(a) * b).astype(BF16)
    y = x + MACARON_SCALE * jnp.dot(g_scr[...], wout_ref[...], preferred_element_type=F32)
    if final_norm:
        out_ref[...] = _rms(y, g2_ref[...])
    else:
        out_ref[...] = y
    if hn_ref is not None:
        hn_ref[...] = _rms(y, g2_ref[...]).astype(hn_ref.dtype)


def _ffn_resident(x, ln, w_in, w_out, gain2, *, hn_dtype, final_norm):
    m, d = x.shape
    wa, wb = w_in
    f = w_out.shape[1]
    tm = min(FFN_TM, m)
    row = lambda i: (i, 0)
    once = pl.Buffered(1)
    out_shape = [jax.ShapeDtypeStruct((m, d), F32)]
    out_specs = [pl.BlockSpec((tm, d), row)]
    if hn_dtype is not None:
        out_shape.append(jax.ShapeDtypeStruct((m, d), hn_dtype))
        out_specs.append(pl.BlockSpec((tm, d), row))
    res = pl.pallas_call(
        functools.partial(_ffn_resident_kernel, tf=FFN_TF, hn_dtype=hn_dtype, final_norm=final_norm),
        grid=(m // tm,),
        in_specs=[
            pl.BlockSpec((tm, d), row),
            pl.BlockSpec((1, d), lambda i: (0, 0)),
            pl.BlockSpec((None, d, f), lambda i: (0, 0, 0), pipeline_mode=once),
            pl.BlockSpec((None, d, f), lambda i: (0, 0, 0), pipeline_mode=once),
            pl.BlockSpec((None, f, d), lambda i: (0, 0, 0), pipeline_mode=once),
            pl.BlockSpec((1, d), lambda i: (0, 0)),
        ],
        out_specs=out_specs,
        out_shape=out_shape,
        scratch_shapes=[pltpu.VMEM((tm, f), BF16)],
        compiler_params=_cparams(("parallel",)),
        name="ffn",
    )(x, ln, wa, wb, w_out, gain2)
    return res[0], (res[1] if hn_dtype is not None else None), None


def _ffn_cast(x, ln, w_in, w_out, layer, gain2, *, hn_dtype, final_norm):
    m, d = x.shape
    f = w_out.shape[1]
    tf = FFN_TF
    nf = f // tf
    assert m <= FFN_TM
    row = lambda i, j: (0, 0)
    out_shape = [jax.ShapeDtypeStruct((m, d), F32)]
    out_specs = [pl.BlockSpec((m, d), row)]
    if hn_dtype is not None:
        out_shape.append(jax.ShapeDtypeStruct((m, d), hn_dtype))
        out_specs.append(pl.BlockSpec((m, d), row))
    out_shape += [jax.ShapeDtypeStruct((1, d, f), BF16)] * 2 + [jax.ShapeDtypeStruct((1, f, d), BF16)]
    out_specs += [pl.BlockSpec((None, d, tf), lambda i, j: (0, 0, j))] * 2
    out_specs += [pl.BlockSpec((None, tf, d), lambda i, j: (0, j, 0))]
    res = pl.pallas_call(
        functools.partial(_ffn_cast_kernel, nf=nf, hn_dtype=hn_dtype, final_norm=final_norm),
        grid=(1, nf),
        in_specs=[
            pl.BlockSpec((m, d), row),
            pl.BlockSpec((1, d), row),
            pl.BlockSpec((None, d, tf), lambda i, j: (layer, 0, j)),
            pl.BlockSpec((None, d, tf), lambda i, j: (layer, 0, nf + j)),
            pl.BlockSpec((None, tf, d), lambda i, j: (layer, j, 0)),
            pl.BlockSpec((1, d), row),
        ],
        out_specs=out_specs,
        out_shape=out_shape,
        scratch_shapes=[pltpu.VMEM((m, d), BF16), pltpu.VMEM((m, d), F32)],
        compiler_params=_cparams(("arbitrary", "arbitrary")),
        name="ffn_cast",
    )(x, ln, w_in, w_in, w_out, gain2)
    res = list(res)
    out = res.pop(0)
    hn = res.pop(0) if hn_dtype is not None else None
    return out, hn, ((res[0], res[1]), res[2])


def _hgrn_gates_kernel(hn_ref, w_ref, lbl_ref, cum_ref, q_ref, k_ref, v_ref, b_ref, sg_ref, *, layer, d, nh):
    h = hn_ref[...]
    logits = lbl_ref[...]
    e = jnp.exp(logits - jnp.max(logits, axis=0, keepdims=True))
    p = e / jnp.sum(e, axis=0, keepdims=True)
    cum = p[0:1]
    for r in range(1, layer + 1):
        cum = cum + p[r:r + 1]
    lb = jnp.clip(cum - p[0:1], 0.0, LB_CLIP)

    def heads(ref, val):
        for hh in range(nh):
            ref[hh] = val[:, hh * HEAD_DIM:(hh + 1) * HEAD_DIM].astype(ref.dtype)

    pq = jnp.dot(h, w_ref[:, 0:d], preferred_element_type=F32)
    heads(q_ref, pq)
    pf = jnp.dot(h, w_ref[:, d:2 * d], preferred_element_type=F32)
    ez = jnp.exp(-jnp.abs(pf))
    r = 1.0 / (1.0 + ez)
    pos = pf >= 0.0
    sig_p = jnp.where(pos, r, ez * r)
    sig_n = jnp.where(pos, ez * r, r)
    fg = lb + (1.0 - lb) * sig_p
    hi, mid, lo = _split3(jnp.log(jnp.maximum(fg, GATE_FLOOR)))
    cum = cum_ref[...]
    dot = functools.partial(jnp.dot, preferred_element_type=F32)
    R = ROW_BLOCK
    for rb in range(h.shape[0] // R):
        rows = slice(rb * R, (rb + 1) * R)
        b = (dot(cum, hi[rows]) + dot(cum, mid[rows])) + dot(cum, lo[rows])
        for hh in range(nh):
            b_ref[hh, rows, :] = b[:, hh * HEAD_DIM:(hh + 1) * HEAD_DIM]
    heads(k_ref, (1.0 - lb) * sig_n)
    pv = jnp.dot(h, w_ref[:, 2 * d:3 * d], preferred_element_type=F32)
    heads(v_ref, pv)
    pg = jnp.dot(h, w_ref[:, 3 * d:4 * d], preferred_element_type=F32)
    heads(sg_ref, pg * _sigmoid(pg))


def _hgrn_gates(hn, w_in, layer_j, lb_logits, layer, cum):
    m, d = hn.shape
    nh = d // HEAD_DIM
    tm = min(512, m)
    hm = lambda dt: jax.ShapeDtypeStruct((nh, m, HEAD_DIM), dt)
    hspec = pl.BlockSpec((nh, tm, HEAD_DIM), lambda i: (0, i, 0))
    return pl.pallas_call(
        functools.partial(_hgrn_gates_kernel, layer=layer, d=d, nh=nh),
        grid=(m // tm,),
        in_specs=[
            pl.BlockSpec((tm, d), lambda i: (i, 0)),
            pl.BlockSpec((None, d, 4 * d), lambda i: (layer_j, 0, 0)),
            pl.BlockSpec(lb_logits.shape, lambda i: (0, 0)),
            pl.BlockSpec(cum.shape, lambda i: (0, 0)),
        ],
        out_specs=[hspec] * 5,
        out_shape=[hm(BF16), hm(BF16), hm(BF16), hm(F32), hm(BF16)],
        compiler_params=_cparams(("parallel",)),
        name="hgrn_gates",
    )(hn, w_in, lb_logits, cum)


def _anchor(b, m):
    R, n = b.shape
    if m >= 8:
        parts = [jnp.broadcast_to(b[blk * 2 * m + m - 1:blk * 2 * m + m, :], (2 * m, n)) for blk in range(R // (2 * m))]
        return parts[0] if len(parts) == 1 else jnp.concatenate(parts, axis=0)
    if m == 1:
        odd = jnp.bitwise_and(lax.broadcasted_iota(jnp.int32, (R, 1), 0), 1) == 1
        return jnp.where(odd, pltpu.roll(b, 1, 0), b)
    b3 = b.reshape(R // 8, 8, n)
    if m == 4:
        a3 = jnp.broadcast_to(b3[:, 3:4, :], b3.shape)
    else:
        sub = lax.broadcasted_iota(jnp.int32, (1, 8, 1), 1)
        a3 = jnp.where(sub < 4, jnp.broadcast_to(b3[:, 1:2, :], b3.shape), jnp.broadcast_to(b3[:, 5:6, :], b3.shape))
    return a3.reshape(R, n)


def _block_end(b, tc):
    R, n = b.shape
    if tc == 8:
        b3 = b.reshape(R // 8, 8, n)
        return jnp.broadcast_to(b3[:, 7:8, :], b3.shape).reshape(R, n)
    parts = [jnp.broadcast_to(b[blk * tc + tc - 1:blk * tc + tc, :], (tc, n)) for blk in range(R // tc)]
    return parts[0] if len(parts) == 1 else jnp.concatenate(parts, axis=0)


def _hgrn_scan_kernel(*refs, tc, nseq, nrb, nt, has_state, n_prev):
    refs = list(refs)
    q_ref, k_ref, v_ref, b_ref, sg_ref, gn_ref, msk_ref = refs[:7]
    del refs[:7]
    s0_ref = refs.pop(0) if has_state else None
    prev_ref = refs.pop(0) if n_prev else None
    o_ref, st_ref, s_scr = refs
    R = ROW_BLOCK
    t_idx = pl.program_id(2)
    dot = functools.partial(jnp.dot, preferred_element_type=F32)
    dn = (((1,), (1,)), ((), ()))
    dn0 = (((0,), (0,)), ((), ()))

    if nseq == 1:
        @pl.when(t_idx == 0)
        def _():
            if has_state:
                s_scr[...] = s0_ref[...].reshape(s_scr.shape)
            else:
                s_scr[...] = jnp.zeros_like(s_scr)

    seq_of_row = lax.broadcasted_iota(jnp.int32, (R, 1), 0) // tc
    gn = gn_ref[...]

    def local_part(rb):
        sl = pl.ds(rb * R, R)
        q = q_ref[sl, :].astype(F32)
        kk = k_ref[sl, :].astype(F32)
        vb = v_ref[sl, :]
        b = b_ref[sl, :]
        scores = lax.dot_general(q_ref[sl, :], k_ref[sl, :], dn, preferred_element_type=F32) * msk_ref[0]
        m, level = 1, 1
        while m < min(tc, 8):
            w = jnp.exp(-jnp.abs(b - _anchor(b, m)))
            s_m = lax.dot_general((q * w).astype(BF16), (kk * w).astype(BF16), dn, preferred_element_type=F32)
            scores = scores + s_m * msk_ref[level]
            m *= 2
            level += 1
        while m < tc:
            nblk = R // (2 * m)
            q_parts, k_parts, spans = [], [], []
            for blk in range(nblk):
                lo, mid, hi = blk * 2 * m, blk * 2 * m + m, (blk + 1) * 2 * m
                anchor = b[mid - 1:mid, :]
                q_parts.append(q[mid:hi] * jnp.exp(b[mid:hi] - anchor))
                k_parts += [kk[lo:mid] * jnp.exp(anchor - b[lo:mid]), jnp.zeros((m, HEAD_DIM), F32)]
                spans.append((lo, mid, hi))
            s_m = lax.dot_general(jnp.concatenate(q_parts, axis=0).astype(BF16),
                                  jnp.concatenate(k_parts, axis=0).astype(BF16), dn, preferred_element_type=F32)
            rows = []
            for blk, (lo, mid, hi) in enumerate(spans):
                piece = s_m[blk * m:(blk + 1) * m]
                if nblk > 1:
                    piece = piece * msk_ref[level, mid:hi, :]
                rows += [scores[lo:mid], scores[mid:hi] + piece]
            scores = jnp.concatenate(rows, axis=0)
            m *= 2
            level += 1
        e_end = _block_end(b, tc)
        qe = q * jnp.exp(b)
        ke = kk * jnp.exp(e_end - b)
        if nseq > 1:
            qe = jnp.concatenate([jnp.where(seq_of_row == i, qe, 0.0) for i in range(nseq)], axis=1)
            ke = jnp.concatenate([jnp.where(seq_of_row == i, ke, 0.0) for i in range(nseq)], axis=1)
        ds = lax.dot_general(ke.astype(BF16), vb, dn0, preferred_element_type=F32)
        decs = []
        for i in range(nseq):
            dec_row = jnp.exp(e_end[i * tc:i * tc + 1, :])
            decs.append(jnp.transpose(jnp.broadcast_to(dec_row, (HEAD_DIM, HEAD_DIM))))
        dec = decs[0] if nseq == 1 else jnp.concatenate(decs, axis=0)
        return jnp.concatenate([scores.astype(BF16), qe.astype(BF16)], axis=1), vb, ds, dec

    parts = [local_part(rb) for rb in range(nrb)]

    s_run = s_scr[...] if nseq == 1 else None
    for rb in range(nrb):
        sq, vb, ds, dec = parts[rb]
        sl = pl.ds(rb * R, R)
        if nseq == 1:
            s_in = s_run
        else:
            s_in = s0_ref[rb * nseq:(rb + 1) * nseq].reshape(nseq * HEAD_DIM, HEAD_DIM)
        o = dot(sq, jnp.concatenate([vb, s_in.astype(BF16)], axis=0))
        s_new = dec * s_in + ds
        on = o * lax.rsqrt(jnp.mean(o * o, axis=-1, keepdims=True) + EPS) * gn
        o_ref[sl, :] = (on * sg_ref[sl, :].astype(F32)).astype(o_ref.dtype)
        if nseq == 1:
            s_run = s_new
        else:
            st_ref[n_prev, rb * nseq:(rb + 1) * nseq] = s_new.reshape(nseq, HEAD_DIM, HEAD_DIM)

    if nseq == 1:
        s_scr[...] = s_run

        @pl.when(t_idx == nt - 1)
        def _():
            st_ref[n_prev] = s_run.reshape(st_ref.shape[1:])
            if n_prev:
                st_ref[0:n_prev] = prev_ref[...]
    elif n_prev:
        st_ref[0:n_prev] = prev_ref[...]


def _level_consts(tc):
    R = ROW_BLOCK
    t = np.arange(R)[:, None]
    s = np.arange(R)[None, :]
    cum = ((s <= t) & (t // tc == s // tc)).astype(np.float32)
    x = t ^ s
    masks = [t == s]
    m = 1
    while m < tc:
        masks.append((t > s) & (x >= m) & (x < 2 * m))
        m *= 2
    return jnp.asarray(cum, BF16), jnp.asarray(np.stack(masks), F32)


def _hgrn_scan(q, k, v, b, sg, gnorm, masks, state, layer_j, prev, *, bsz, seq):
    nh, m, _ = q.shape
    R = ROW_BLOCK
    has_state = state is not None
    n_prev = 0 if prev is None else prev.shape[0]
    if seq >= R:
        tc, nseq = R, 1
        tt = min(HGRN_TT, seq)
        nt = seq // tt
        nb = 1
        grid = (nh, bsz, nt)
        row_map = lambda h, b, t: (h, b * nt + t, 0)
    else:
        assert has_state
        tc, nseq = seq, R // seq
        tt = min(HGRN_TT_SHORT, m)
        nt = 1
        nb = tt // seq
        grid = (nh, m // tt, 1)
        row_map = lambda h, b, t: (h, b, 0)
    rspec = pl.BlockSpec((None, tt, HEAD_DIM), row_map)
    const2 = lambda h, b, t: (0, 0)
    in_specs = [rspec] * 5 + [pl.BlockSpec((1, HEAD_DIM), const2), pl.BlockSpec(masks.shape, lambda h, b, t: (0, 0, 0))]
    args = [q, k, v, b, sg, gnorm, masks]
    if has_state:
        in_specs.append(pl.BlockSpec((None, nb, None, HEAD_DIM, HEAD_DIM), lambda h, b, t: (layer_j, b, h, 0, 0)))
        args.append(state)
    st_map = lambda h, b, t: (0, b, h, 0, 0)
    if n_prev:
        in_specs.append(pl.BlockSpec((n_prev, nb, None, HEAD_DIM, HEAD_DIM), st_map))
        args.append(prev)
    return pl.pallas_call(
        functools.partial(_hgrn_scan_kernel, tc=tc, nseq=nseq, nrb=tt // R, nt=nt, has_state=has_state,
                          n_prev=n_prev),
        grid=grid,
        in_specs=in_specs,
        out_specs=[rspec, pl.BlockSpec((n_prev + 1, nb, None, HEAD_DIM, HEAD_DIM), st_map)],
        out_shape=[jax.ShapeDtypeStruct((nh, m, HEAD_DIM), BF16),
                   jax.ShapeDtypeStruct((n_prev + 1, bsz, nh, HEAD_DIM, HEAD_DIM), F32)],
        scratch_shapes=[pltpu.VMEM((HEAD_DIM, HEAD_DIM), F32)],
        compiler_params=_cparams(("parallel", "parallel", "arbitrary")),
        name="hgrn_scan",
    )(*args)


def _hgrn_out_kernel(x_ref, og_ref, w_ref, out_ref, *, nh):
    og = jnp.concatenate([og_ref[hh] for hh in range(nh)], axis=-1)
    out_ref[...] = x_ref[...] + jnp.dot(og, w_ref[...], preferred_element_type=F32)


def _hgrn_out(x, og, w_out, layer_j):
    m, d = x.shape
    nh = og.shape[0]
    tm = min(1024, m)
    return pl.pallas_call(
        functools.partial(_hgrn_out_kernel, nh=nh),
        grid=(m // tm,),
        in_specs=[
            pl.BlockSpec((tm, d), lambda i: (i, 0)),
            pl.BlockSpec((nh, tm, HEAD_DIM), lambda i: (0, i, 0)),
            pl.BlockSpec((None, d, d), lambda i: (layer_j, 0, 0)),
        ],
        out_specs=pl.BlockSpec((tm, d), lambda i: (i, 0)),
        out_shape=jax.ShapeDtypeStruct((m, d), F32),
        compiler_params=_cparams(("parallel",)),
        name="hgrn_out",
    )(x, og, w_out)


def _split3(a):
    hi = a.astype(BF16)
    r1 = a - hi.astype(F32)
    mid = r1.astype(BF16)
    lo = (r1 - mid.astype(F32)).astype(BF16)
    return hi, mid, lo


def _dot_precise(a, b):
    ah, am, al = _split3(a)
    bh, bm, bl = _split3(b)
    d = functools.partial(jnp.dot, preferred_element_type=F32)
    return (d(ah, bh) + (d(ah, bm) + d(am, bh))) + ((d(am, bm) + d(ah, bl)) + d(al, bh))


def _s5_prep_kernel(arc_ref, aic_ref, arr_ref, air_ref, ldt_ref, x1_ref, x2_ref, y1_ref, y2_ref, plt_ref, pls_ref,
                    w1_ref, w2_ref, w3_ref, pw_ref):
    L, GC, P, GL = S5_L, S5_GC, S5_P, S5_GL
    W = L * GC
    TW = L * GL * GC
    lane_t = lax.broadcasted_iota(jnp.int32, (1, W), 1) // GC
    row_s = lax.broadcasted_iota(jnp.int32, (W, 1), 0) // GC
    sgn_l = jnp.where(lax.broadcasted_iota(jnp.int32, (1, 2 * P), 1) < P, -1.0, 1.0)
    sgn_s = jnp.where(lax.broadcasted_iota(jnp.int32, (2 * P, 1), 0) < P, 1.0, -1.0)
    row16 = lax.broadcasted_iota(jnp.int32, (16, 1), 0)
    dot = functools.partial(jnp.dot, preferred_element_type=F32)

    def cmul(ar, ai, br, bi):
        return ar * br - ai * bi, ar * bi + ai * br

    bases = []
    pw = jnp.zeros((16, GL * 2 * P), F32)
    for g in range(GL):
        dt = jnp.exp(ldt_ref[g])
        a_re_c, a_im_c = arc_ref[g], aic_ref[g]
        er = jnp.exp(a_re_c * dt)
        lbr_c, lbi_c = er * jnp.cos(a_im_c * dt), er * jnp.sin(a_im_c * dt)
        a_re_r, a_im_r = arr_ref[g], air_ref[g]
        er_r = jnp.exp(a_re_r * dt)
        lbr_r, lbi_r = er_r * jnp.cos(a_im_r * dt), er_r * jnp.sin(a_im_r * dt)
        den = a_re_r * a_re_r + a_im_r * a_im_r
        nr, ni = lbr_r - 1.0, lbi_r
        f_r = (nr * a_re_r + ni * a_im_r) / den
        f_i = (ni * a_re_r - nr * a_im_r) / den

        pc = [(jnp.ones_like(lbr_c), jnp.zeros_like(lbr_c))]
        pr = [(jnp.ones_like(lbr_r), jnp.zeros_like(lbr_r))]
        for _ in range(L):
            pc.append(cmul(pc[-1][0], pc[-1][1], lbr_c, lbi_c))
            pr.append(cmul(pr[-1][0], pr[-1][1], lbr_r, lbi_r))

        q_re = jnp.zeros((2 * P, W), F32)
        q_im = jnp.zeros((2 * P, W), F32)
        for t in range(L):
            sel = lane_t == t
            q_re = jnp.where(sel, pc[t][0], q_re)
            q_im = jnp.where(sel, pc[t][1], q_im)
        y1, y2 = y1_ref[g], y2_ref[g]
        mp = sgn_s * (q_re * y1) - q_im * y2
        q1_re, q1_im = cmul(q_re, q_im, lbr_c, lbi_c)
        w3g = (sgn_s * (q1_re * y1) - q1_im * y2).astype(BF16)
        z3 = dot(w3g, plt_ref[g]).astype(BF16)
        for ri in range(2):
            w3_ref[ri * GL * P + g * P:ri * GL * P + (g + 1) * P, :] = z3[ri * P:(ri + 1) * P]

        cf_re = jnp.zeros((W, 2 * P), F32)
        cf_im = jnp.zeros((W, 2 * P), F32)
        for s in range(L):
            sel = row_s == s
            cf_re = jnp.where(sel, pr[L - 1 - s][0], cf_re)
            cf_im = jnp.where(sel, pr[L - 1 - s][1], cf_im)
        cf_re, cf_im = cmul(cf_re, cf_im, f_r, f_i)
        x1, x2 = x1_ref[g], x2_ref[g]
        w2g = (cf_re * x1 + (cf_im * sgn_l) * x2).astype(BF16)
        v2 = dot(w2g, pls_ref[g]).astype(BF16)
        for s in range(L):
            w2_ref[s * GL * GC + g * GC:s * GL * GC + (g + 1) * GC, :] = v2[s * GC:(s + 1) * GC]

        bbp = f_r * x1[0:GC] + (f_i * sgn_l) * x2[0:GC]
        base = _dot_precise(bbp, mp).astype(BF16)
        bases.append(dot(base, plt_ref[g]))

        tab = jnp.zeros((16, 2 * P), F32)
        cur = pr[L]
        for k in range(8):
            tab = jnp.where(row16 == k, cur[0], tab)
            tab = jnp.where(row16 == 8 + k, cur[1] * sgn_l, tab)
            cur = cmul(cur[0], cur[1], cur[0], cur[1])
        th, tm_, tl = _split3(tab)
        pls = pls_ref[g]
        pw = pw + ((dot(th, pls) + dot(tm_, pls)) + dot(tl, pls))

    base8 = jnp.concatenate(bases, axis=0)
    lane = lax.broadcasted_iota(jnp.int32, (1, TW), 1)
    blk = GL * GC
    w1_ref[0:blk, :] = base8.astype(BF16)
    for s in range(1, L):
        w1_ref[s * blk:(s + 1) * blk, :] = jnp.where(lane >= s * blk, pltpu.roll(base8, s * blk, 1), 0.0).astype(BF16)
    pw_ref[...] = pw


def _s5_placements():
    L, GC, P, GL = S5_L, S5_GC, S5_P, S5_GL
    plt = np.zeros((GL, L * GC, L * GL * GC), np.float32)
    pls = np.zeros((GL, 2 * P, 2 * GL * P), np.float32)
    for gl in range(GL):
        for t in range(L):
            for c in range(GC):
                plt[gl, t * GC + c, t * GL * GC + gl * GC + c] = 1.0
        for ri in range(2):
            for p in range(P):
                pls[gl, ri * P + p, ri * GL * P + gl * P + p] = 1.0
    return jnp.asarray(plt, BF16), jnp.asarray(pls, BF16)


def _s5_prep(a_re, a_im, log_dt, b_re, b_im, c_re, c_im):
    g, p = a_re.shape
    L, GC, GL = S5_L, S5_GC, S5_GL
    W = L * GC
    TW = L * GL * GC
    SW = 2 * GL * p
    nblk = g // GL
    a_re2 = jnp.concatenate([a_re, a_re], axis=1)
    a_im2 = jnp.concatenate([a_im, a_im], axis=1)
    arc, aic = a_re2[:, :, None], a_im2[:, :, None]
    arr, air = a_re2[:, None, :], a_im2[:, None, :]
    ldt = log_dt[:, None, None]
    btr, bti = jnp.transpose(b_re, (0, 2, 1)), jnp.transpose(b_im, (0, 2, 1))
    x1 = jnp.tile(jnp.concatenate([btr, bti], axis=2), (1, L, 1))
    x2 = jnp.tile(jnp.concatenate([bti, btr], axis=2), (1, L, 1))
    ctr, cti = jnp.transpose(c_re, (0, 2, 1)), jnp.transpose(c_im, (0, 2, 1))
    y1 = jnp.tile(jnp.concatenate([ctr, cti], axis=1), (1, 1, L))
    y2 = jnp.tile(jnp.concatenate([cti, ctr], axis=1), (1, 1, L))
    plt, pls = _s5_placements()

    def spec(shape):
        return pl.BlockSpec((GL,) + shape, lambda i: (i, 0, 0))

    const3 = lambda i: (0, 0, 0)
    out3 = lambda i: (i, 0, 0)
    return pl.pallas_call(
        _s5_prep_kernel,
        grid=(nblk,),
        in_specs=[spec((2 * p, 1)), spec((2 * p, 1)), spec((1, 2 * p)), spec((1, 2 * p)), spec((1, 1)),
                  spec((W, 2 * p)), spec((W, 2 * p)), spec((2 * p, W)), spec((2 * p, W)),
                  pl.BlockSpec(plt.shape, const3), pl.BlockSpec(pls.shape, const3)],
        out_specs=[pl.BlockSpec((None, TW, TW), out3), pl.BlockSpec((None, TW, SW), out3),
                   pl.BlockSpec((None, SW, TW), out3), pl.BlockSpec((None, 16, SW), out3)],
        out_shape=[jax.ShapeDtypeStruct((nblk, TW, TW), BF16), jax.ShapeDtypeStruct((nblk, TW, SW), BF16),
                   jax.ShapeDtypeStruct((nblk, SW, TW), BF16), jax.ShapeDtypeStruct((nblk, 16, SW), F32)],
        compiler_params=_cparams(("parallel",)),
        name="s5_prep",
    )(arc, aic, arr, air, ldt, x1, x2, y1, y2, plt, pls)


def _s5_core_kernel(*refs, nrows, has_state):
    if has_state:
        hn_ref, w1_ref, w2_ref, w3_ref, pw_ref, s0_ref, y_ref, fin_ref = refs
    else:
        hn_ref, w1_ref, w2_ref, w3_ref, pw_ref, y_ref, fin_ref, x_scr, p_scr, s1_scr, sin_scr = refs
    L = S5_L
    half = S5_GL * S5_P
    dot = functools.partial(jnp.dot, preferred_element_type=F32)

    ns = half // 128
    strips = lambda a: [a[:, s * 128:(s + 1) * 128] for s in range(2 * ns)]
    join = lambda parts: jnp.concatenate(parts, axis=1)

    def scale_add(a, k, c):
        out = [None] * (2 * ns)
        for st in range(ns):
            wr = pw_ref[k:k + 1, st * 128:(st + 1) * 128]
            wi = pw_ref[8 + k:9 + k, half + st * 128:half + (st + 1) * 128]
            ar, ai = a[st], a[ns + st]
            out[st] = c[st] + (ar * wr - ai * wi)
            out[ns + st] = c[ns + st] + (ar * wi + ai * wr)
        return out

    u = jnp.concatenate([hn_ref[pl.ds(s, nrows, stride=L), :] for s in range(L)], axis=1).astype(BF16)
    x = dot(u, w2_ref[...])
    if has_state:
        s_in = s0_ref[...]
        fin_ref[...] = join(scale_add(strips(s_in), 0, strips(x)))
    else:
        n1, n2 = nrows // 2, nrows // 4
        every = range(2 * ns)

        def put(ref, parts, start=None, n=None):
            for s in every:
                if start is None:
                    ref[s] = parts[s]
                else:
                    ref[s, pl.ds(start, n, stride=2), :] = parts[s]

        def alternate(ref, start, n):
            return [ref[s, pl.ds(start, n, stride=2), :] for s in every]

        put(x_scr, strips(x))
        e0, o0 = alternate(x_scr, 0, n1), alternate(x_scr, 1, n1)
        put(p_scr, scale_add(e0, 0, o0))
        e1, o1 = alternate(p_scr, 0, n2), alternate(p_scr, 1, n2)
        z = scale_add(e1, 1, o1)
        row = lax.broadcasted_iota(jnp.int32, (n2, 1), 0)

        def shift(parts, sh):
            if sh % 8 == 0:
                return [jnp.concatenate([jnp.zeros((sh, 128), F32), a[:n2 - sh]], axis=0) for a in parts]
            return [jnp.where(row >= sh, pltpu.roll(a, sh, 0), 0.0) for a in parts]

        sh, k = 1, 2
        while sh < n2:
            z = scale_add(shift(z, sh), k, z)
            sh *= 2
            k += 1
        fin_ref[...] = join([a[n2 - 1:n2, :] for a in z])
        s2 = shift(z, 1)
        put(s1_scr, s2, 0, n2)
        put(s1_scr, scale_add(s2, 1, e1), 1, n2)
        s1 = [s1_scr[s] for s in every]
        put(sin_scr, s1, 0, n1)
        put(sin_scr, scale_add(s1, 0, e0), 1, n1)
        s_in = join([sin_scr[s] for s in every])
    y = dot(u, w1_ref[...]) + dot(s_in.astype(BF16), w3_ref[...])
    lanes = y_ref.shape[1]
    for t in range(L):
        y_ref[pl.ds(t, nrows, stride=L), :] = y[:, t * lanes:(t + 1) * lanes]


def _s5_core(hn, w1, w2, w3, pw, s0, *, bsz, seq):
    m, d = hn.shape
    L = S5_L
    nblk, tw, sw = w2.shape
    lanes = d // nblk
    has_state = s0 is not None
    assert has_state == (seq == L) and seq % L == 0 and seq // L <= 256
    if has_state:
        tile, nb = m, 1
        fin_shape = jax.ShapeDtypeStruct((bsz, nblk * sw), F32)
        fin_spec = pl.BlockSpec((bsz, sw), lambda g, b: (0, g))
    else:
        tile, nb = seq, bsz
        fin_shape = jax.ShapeDtypeStruct((bsz, 1, nblk * sw), F32)
        fin_spec = pl.BlockSpec((None, 1, sw), lambda g, b: (b, 0, g))
    nrows = tile // L
    wmap = lambda g, b: (g, 0, 0)
    in_specs = [
        pl.BlockSpec((tile, lanes), lambda g, b: (b, g)),
        pl.BlockSpec((None, tw, tw), wmap), pl.BlockSpec((None, tw, sw), wmap),
        pl.BlockSpec((None, sw, tw), wmap), pl.BlockSpec((None, 16, sw), wmap),
    ]
    args = [hn, w1, w2, w3, pw]
    if has_state:
        in_specs.append(pl.BlockSpec((bsz, sw), lambda g, b: (0, g)))
        args.append(s0)
    return pl.pallas_call(
        functools.partial(_s5_core_kernel, nrows=nrows, has_state=has_state),
        grid=(nblk, nb),
        in_specs=in_specs,
        out_specs=[pl.BlockSpec((tile, lanes), lambda g, b: (b, g)), fin_spec],
        out_shape=[jax.ShapeDtypeStruct((m, d), F32), fin_shape],
        scratch_shapes=[] if has_state else [pltpu.VMEM((sw // 128, nrows // r, 128), F32) for r in (1, 2, 2, 1)],
        compiler_params=_cparams(("parallel", "parallel")),
        name="s5_core",
    )(*args)


def _gelu_tanh(x):
    return 0.5 * x * (1.0 + jnp.tanh(0.7978845608028654 * (x + 0.044715 * (x * x * x))))


def _s5_out_kernel(x_ref, ln_ref, y_ref, d_ref, w_ref, out_ref, *, d):
    x = x_ref[...]
    h = _rms(x, ln_ref[...])
    y = y_ref[...].astype(F32) + d_ref[...] * h
    z = _gelu_tanh(y).astype(BF16)
    ag = jnp.dot(z, w_ref[...], preferred_element_type=F32)
    out_ref[...] = x + ag[:, :d] * _sigmoid(ag[:, d:])


def _s5_out(x, ln, y, d_skip, w_glu, layer_j):
    m, d = x.shape
    tm = min(1024, m)
    return pl.pallas_call(
        functools.partial(_s5_out_kernel, d=d),
        grid=(m // tm,),
        in_specs=[
            pl.BlockSpec((tm, d), lambda i: (i, 0)),
            pl.BlockSpec((1, d), lambda i: (0, 0)),
            pl.BlockSpec((tm, d), lambda i: (i, 0)),
            pl.BlockSpec((1, d), lambda i: (0, 0)),
            pl.BlockSpec((None, d, 2 * d), lambda i: (layer_j, 0, 0)),
        ],
        out_specs=pl.BlockSpec((tm, d), lambda i: (i, 0)),
        out_shape=jax.ShapeDtypeStruct((m, d), F32),
        compiler_params=_cparams(("parallel",)),
        name="s5_out",
    )(x, ln, y, d_skip, w_glu)


def _unpack_state(fin, bsz):
    f = fin.reshape(bsz, -1, 2, S5_GL, S5_P)
    return f[:, :, 0].reshape(bsz, -1, S5_P), f[:, :, 1].reshape(bsz, -1, S5_P)


def _layer(i, st, wts, w_bf):
    j = i // 2
    depth = wts["ln_ffn1"].shape[0]
    x, bsz, seq = st["x"], st["bsz"], st["seq"]
    is_hgrn = i % 2 == 0

    def ffn(name, x, ln, gain2, **kw):
        key = (name, i)
        kw = dict(dict(hn_dtype=None, final_norm=False), **kw)
        if key in w_bf:
            w_in, w_out = w_bf[key]
            out, hn, _ = _ffn_resident(x, ln, w_in, w_out, gain2, **kw)
        else:
            out, hn, w_bf[key] = _ffn_cast(x, ln, wts[name + "_w_in"], wts[name + "_w_out"], i, gain2, **kw)
        return out, hn

    x, hn = ffn("ffn1", x, wts["ln_ffn1"][i:i + 1], wts["ln_mix"][i:i + 1], hn_dtype=BF16 if is_hgrn else F32)
    if is_hgrn:
        cum, masks = _level_consts(min(seq, ROW_BLOCK))
        q, k, v, b, sg = _hgrn_gates(hn, wts["hgrn_w_in"], j, wts["hgrn_lb_logits"], i, cum)
        og, st["new_hgrn"] = _hgrn_scan(q, k, v, b, sg, wts["hgrn_gnorm"][j:j + 1], masks, st["hgrn"], j,
                                        st["new_hgrn"], bsz=bsz, seq=seq)
        x = _hgrn_out(x, og, wts["hgrn_w_out"], j)
    else:
        w1, w2, w3, pw = wts["s5_mats"][j]
        s0 = None
        if st["re"] is not None:
            pack = lambda a: a[j].reshape(bsz, -1, 1, S5_GL, S5_P)
            s0 = jnp.concatenate([pack(st["re"]), pack(st["im"])], axis=2).reshape(bsz, -1)
        y, fin = _s5_core(hn, w1, w2, w3, pw, s0, bsz=bsz, seq=seq)
        f_re, f_im = _unpack_state(fin, bsz)
        st["new_re"].append(f_re)
        st["new_im"].append(f_im)
        x = _s5_out(x, wts["ln_mix"][i:i + 1], y, wts["s5_d"][j:j + 1], wts["s5_w_glu"], j)
    last = i == depth - 1
    gain2 = wts["ln_final"] if last else wts["ln_ffn2"][i:i + 1]
    x, _ = ffn("ffn2", x, wts["ln_ffn2"][i:i + 1], gain2, final_norm=last)
    st["x"] = x


def kernel(x_prompt, x_sample, state_hgrn, state_s5_re, state_s5_im, ln_ffn1, ffn1_w_in, ffn1_w_out, ln_mix, ln_ffn2, ffn2_w_in, ffn2_w_out, hgrn_lb_logits, hgrn_w_in, hgrn_gnorm, hgrn_w_out, s5_a_re, s5_a_im, s5_log_dt, s5_b_re, s5_b_im, s5_c_re, s5_c_im, s5_d, s5_w_glu, ln_final):
    depth, d = ln_ffn1.shape
    wts = dict(
        ln_ffn1=ln_ffn1, ln_mix=ln_mix, ln_ffn2=ln_ffn2, ln_final=ln_final.reshape(1, -1),
        ffn1_w_in=ffn1_w_in, ffn1_w_out=ffn1_w_out, ffn2_w_in=ffn2_w_in, ffn2_w_out=ffn2_w_out,
        hgrn_lb_logits=hgrn_lb_logits, hgrn_w_in=hgrn_w_in.astype(BF16), hgrn_gnorm=hgrn_gnorm,
        hgrn_w_out=hgrn_w_out.astype(BF16), s5_d=s5_d, s5_w_glu=s5_w_glu.astype(BF16),
        s5_mats=[_s5_prep(s5_a_re[j], s5_a_im[j], s5_log_dt[j], s5_b_re[j], s5_b_im[j], s5_c_re[j], s5_c_im[j])
                 for j in range(s5_a_re.shape[0])],
    )

    def stream(x3, hgrn, re, im):
        bsz, seq, _ = x3.shape
        return dict(x=x3.reshape(bsz * seq, d), bsz=bsz, seq=seq, hgrn=hgrn, re=re, im=im,
                    new_hgrn=None, new_re=[], new_im=[])

    sample = stream(x_sample, state_hgrn, state_s5_re, state_s5_im)
    prompt = stream(x_prompt, None, None, None)
    w_bf = {}
    for i in range(depth):
        _layer(i, sample, wts, w_bf)
        _layer(i, prompt, wts, w_bf)

    def outs(st, x3):
        return (st["x"].reshape(x3.shape), st["new_hgrn"], jnp.stack(st["new_re"]), jnp.stack(st["new_im"]))

    y_p, hg_p, re_p, im_p = outs(prompt, x_prompt)
    y_s, hg_s, re_s, im_s = outs(sample, x_sample)
    return (y_p, y_s, hg_p, re_p, im_p, hg_s, re_s, im_s)
```

```python
import functools

import numpy as np
import jax
import jax.numpy as jnp
from jax import lax
from jax.experimental import pallas as pl
from jax.experimental.pallas import tpu as pltpu

F32 = jnp.float32
BF16 = jnp.bfloat16

EPS = 1e-6
GATE_FLOOR = 1e-30
MACARON_SCALE = 0.5
LB_CLIP = 1.0 - 1e-4

HEAD_DIM = 128
S5_GC = 16
S5_P = 64
S5_L = 8
S5_GL = 8
ROW_BLOCK = 128
FFN_TM = 1024
FFN_TF = 256
HGRN_TT = 2048
HGRN_GATES_TM = 1024
HGRN_TT_SHORT = 512
VMEM_LIMIT = 52 * 1024 * 1024


def _cparams(sem):
    return pltpu.CompilerParams(dimension_semantics=sem, vmem_limit_bytes=VMEM_LIMIT)


def _rms(x, gain):
    return x * lax.rsqrt(jnp.mean(x * x, axis=-1, keepdims=True) + EPS) * gain


def _sigmoid(x):
    return 1.0 / (1.0 + jnp.exp(-x))


def _ffn_cast_kernel(x_ref, ln_ref, wa_ref, wb_ref, wout_ref, g2_ref, *rest, nf, hn_dtype, final_norm):
    rest = list(rest)
    out_ref = rest.pop(0)
    hn_ref = rest.pop(0) if hn_dtype is not None else None
    wa_bf_ref, wb_bf_ref, wout_bf_ref, h_scr, acc_scr = rest
    j = pl.program_id(1)

    @pl.when(j == 0)
    def _():
        h_scr[...] = _rms(x_ref[...], ln_ref[...]).astype(BF16)
        acc_scr[...] = jnp.zeros_like(acc_scr)

    wa, wb, wo = wa_ref[...].astype(BF16), wb_ref[...].astype(BF16), wout_ref[...].astype(BF16)
    wa_bf_ref[...] = wa
    wb_bf_ref[...] = wb
    wout_bf_ref[...] = wo
    h = h_scr[...]
    a = jnp.dot(h, wa, preferred_element_type=F32)
    b = jnp.dot(h, wb, preferred_element_type=F32)
    g = (a * _sigmoid(a) * b).astype(BF16)
    acc_scr[...] += jnp.dot(g, wo, preferred_element_type=F32)

    @pl.when(j == nf - 1)
    def _():
        y = x_ref[...] + MACARON_SCALE * acc_scr[...]
        if final_norm:
            out_ref[...] = _rms(y, g2_ref[...])
        else:
            out_ref[...] = y
        if hn_ref is not None:
            hn_ref[...] = _rms(y, g2_ref[...]).astype(hn_ref.dtype)


def _ffn_resident_kernel(x_ref, ln_ref, wa_ref, wb_ref, wout_ref, g2_ref, *rest, tf, hn_dtype, final_norm):
    rest = list(rest)
    out_ref = rest.pop(0)
    hn_ref = rest.pop(0) if hn_dtype is not None else None
    g_scr, = rest
    x = x_ref[...]
    h = _rms(x, ln_ref[...]).astype(BF16)
    for j in range(g_scr.shape[1] // tf):
        cols = slice(j * tf, (j + 1) * tf)
        a = jnp.dot(h, wa_ref[:, cols], preferred_element_type=F32)
        b = jnp.dot(h, wb_ref[:, cols], preferred_element_type=F32)
        g_scr[:, cols] = (a * _sigmoid(a) * b).astype(BF16)
    y = x + MACARON_SCALE * jnp.dot(g_scr[...], wout_ref[...], preferred_element_type=F32)
    if final_norm:
        out_ref[...] = _rms(y, g2_ref[...])
    else:
        out_ref[...] = y
    if hn_ref is not None:
        hn_ref[...] = _rms(y, g2_ref[...]).astype(hn_ref.dtype)


def _ffn_resident(x, ln, w_in, w_out, gain2, *, hn_dtype, final_norm):
    m, d = x.shape
    wa, wb = w_in
    f = w_out.shape[1]
    tm = min(FFN_TM, m)
    row = lambda i: (i, 0)
    once = pl.Buffered(1)
    out_shape = [jax.ShapeDtypeStruct((m, d), F32)]
    out_specs = [pl.BlockSpec((tm, d), row)]
    if hn_dtype is not None:
        out_shape.append(jax.ShapeDtypeStruct((m, d), hn_dtype))
        out_specs.append(pl.BlockSpec((tm, d), row))
    res = pl.pallas_call(
        functools.partial(_ffn_resident_kernel, tf=FFN_TF, hn_dtype=hn_dtype, final_norm=final_norm),
        grid=(m // tm,),
        in_specs=[
            pl.BlockSpec((tm, d), row),
            pl.BlockSpec((1, d), lambda i: (0, 0)),
            pl.BlockSpec((None, d, f), lambda i: (0, 0, 0), pipeline_mode=once),
            pl.BlockSpec((None, d, f), lambda i: (0, 0, 0), pipeline_mode=once),
            pl.BlockSpec((None, f, d), lambda i: (0, 0, 0), pipeline_mode=once),
            pl.BlockSpec((1, d), lambda i: (0, 0)),
        ],
        out_specs=out_specs,
        out_shape=out_shape,
        scratch_shapes=[pltpu.VMEM((tm, f), BF16)],
        compiler_params=_cparams(("parallel",)),
        name="ffn",
    )(x, ln, wa, wb, w_out, gain2)
    return res[0], (res[1] if hn_dtype is not None else None), None


def _ffn_cast(x, ln, w_in, w_out, layer, gain2, *, hn_dtype, final_norm):
    m, d = x.shape
    f = w_out.shape[1]
    tf = FFN_TF
    nf = f // tf
    assert m <= FFN_TM
    row = lambda i, j: (0, 0)
    out_shape = [jax.ShapeDtypeStruct((m, d), F32)]
    out_specs = [pl.BlockSpec((m, d), row)]
    if hn_dtype is not None:
        out_shape.append(jax.ShapeDtypeStruct((m, d), hn_dtype))
        out_specs.append(pl.BlockSpec((m, d), row))
    out_shape += [jax.ShapeDtypeStruct((1, d, f), BF16)] * 2 + [jax.ShapeDtypeStruct((1, f, d), BF16)]
    out_specs += [pl.BlockSpec((None, d, tf), lambda i, j: (0, 0, j))] * 2
    out_specs += [pl.BlockSpec((None, tf, d), lambda i, j: (0, j, 0))]
    res = pl.pallas_call(
        functools.partial(_ffn_cast_kernel, nf=nf, hn_dtype=hn_dtype, final_norm=final_norm),
        grid=(1, nf),
        in_specs=[
            pl.BlockSpec((m, d), row),
            pl.BlockSpec((1, d), row),
            pl.BlockSpec((None, d, tf), lambda i, j: (layer, 0, j)),
            pl.BlockSpec((None, d, tf), lambda i, j: (layer, 0, nf + j)),
            pl.BlockSpec((None, tf, d), lambda i, j: (layer, j, 0)),
            pl.BlockSpec((1, d), row),
        ],
        out_specs=out_specs,
        out_shape=out_shape,
        scratch_shapes=[pltpu.VMEM((m, d), BF16), pltpu.VMEM((m, d), F32)],
        compiler_params=_cparams(("arbitrary", "arbitrary")),
        name="ffn_cast",
    )(x, ln, w_in, w_in, w_out, gain2)
    res = list(res)
    out = res.pop(0)
    hn = res.pop(0) if hn_dtype is not None else None
    return out, hn, ((res[0], res[1]), res[2])


def _hgrn_gates_kernel(hn_ref, w_ref, lbl_ref, cum_ref, q_ref, k_ref, v_ref, b_ref, sg_ref, *, layer, d, nh):
    h = hn_ref[...]
    logits = lbl_ref[...]
    e = jnp.exp(logits - jnp.max(logits, axis=0, keepdims=True))
    p = e / jnp.sum(e, axis=0, keepdims=True)
    cum = p[0:1]
    for r in range(1, layer + 1):
        cum = cum + p[r:r + 1]
    lb = jnp.clip(cum - p[0:1], 0.0, LB_CLIP)

    def heads(ref, val):
        for hh in range(nh):
            ref[hh] = val[:, hh * HEAD_DIM:(hh + 1) * HEAD_DIM].astype(ref.dtype)

    pq = jnp.dot(h, w_ref[:, 0:d], preferred_element_type=F32)
    heads(q_ref, pq)
    pf = jnp.dot(h, w_ref[:, d:2 * d], preferred_element_type=F32)
    ez = jnp.exp(-jnp.abs(pf))
    r = 1.0 / (1.0 + ez)
    pos = pf >= 0.0
    sig_p = jnp.where(pos, r, ez * r)
    sig_n = jnp.where(pos, ez * r, r)
    fg = lb + (1.0 - lb) * sig_p
    hi, mid, lo = _split3(jnp.log(jnp.maximum(fg, GATE_FLOOR)))
    cum = cum_ref[...]
    dot = functools.partial(jnp.dot, preferred_element_type=F32)
    R = ROW_BLOCK
    for rb in range(h.shape[0] // R):
        rows = slice(rb * R, (rb + 1) * R)
        b = (dot(cum, hi[rows]) + dot(cum, mid[rows])) + dot(cum, lo[rows])
        for hh in range(nh):
            b_ref[hh, rows, :] = b[:, hh * HEAD_DIM:(hh + 1) * HEAD_DIM]
    heads(k_ref, (1.0 - lb) * sig_n)
    pv = jnp.dot(h, w_ref[:, 2 * d:3 * d], preferred_element_type=F32)
    heads(v_ref, pv)
    pg = jnp.dot(h, w_ref[:, 3 * d:4 * d], preferred_element_type=F32)
    heads(sg_ref, pg * _sigmoid(pg))


def _hgrn_gates(hn, w_in, layer_j, lb_logits, layer, cum):
    m, d = hn.shape
    nh = d // HEAD_DIM
    tm = min(HGRN_GATES_TM, m)
    hm = lambda dt: jax.ShapeDtypeStruct((nh, m, HEAD_DIM), dt)
    hspec = pl.BlockSpec((nh, tm, HEAD_DIM), lambda i: (0, i, 0))
    return pl.pallas_call(
        functools.partial(_hgrn_gates_kernel, layer=layer, d=d, nh=nh),
        grid=(m // tm,),
        in_specs=[
            pl.BlockSpec((tm, d), lambda i: (i, 0)),
            pl.BlockSpec((None, d, 4 * d), lambda i: (layer_j, 0, 0)),
            pl.BlockSpec(lb_logits.shape, lambda i: (0, 0)),
            pl.BlockSpec(cum.shape, lambda i: (0, 0)),
        ],
        out_specs=[hspec] * 5,
        out_shape=[hm(BF16), hm(BF16), hm(BF16), hm(F32), hm(BF16)],
        compiler_params=_cparams(("parallel",)),
        name="hgrn_gates",
    )(hn, w_in, lb_logits, cum)


def _anchor(b, m):
    R, n = b.shape
    if m >= 8:
        parts = [jnp.broadcast_to(b[blk * 2 * m + m - 1:blk * 2 * m + m, :], (2 * m, n)) for blk in range(R // (2 * m))]
        return parts[0] if len(parts) == 1 else jnp.concatenate(parts, axis=0)
    if m == 1:
        odd = jnp.bitwise_and(lax.broadcasted_iota(jnp.int32, (R, 1), 0), 1) == 1
        return jnp.where(odd, pltpu.roll(b, 1, 0), b)
    b3 = b.reshape(R // 8, 8, n)
    if m == 4:
        a3 = jnp.broadcast_to(b3[:, 3:4, :], b3.shape)
    else:
        sub = lax.broadcasted_iota(jnp.int32, (1, 8, 1), 1)
        a3 = jnp.where(sub < 4, jnp.broadcast_to(b3[:, 1:2, :], b3.shape), jnp.broadcast_to(b3[:, 5:6, :], b3.shape))
    return a3.reshape(R, n)


def _block_end(b, tc):
    R, n = b.shape
    if tc == 8:
        b3 = b.reshape(R // 8, 8, n)
        return jnp.broadcast_to(b3[:, 7:8, :], b3.shape).reshape(R, n)
    parts = [jnp.broadcast_to(b[blk * tc + tc - 1:blk * tc + tc, :], (tc, n)) for blk in range(R // tc)]
    return parts[0] if len(parts) == 1 else jnp.concatenate(parts, axis=0)


def _hgrn_scan_kernel(*refs, tc, nseq, nrb, nt, has_state, n_prev):
    refs = list(refs)
    q_ref, k_ref, v_ref, b_ref, sg_ref, gn_ref, msk_ref = refs[:7]
    del refs[:7]
    s0_ref = refs.pop(0) if has_state else None
    prev_ref = refs.pop(0) if n_prev else None
    o_ref, st_ref, s_scr = refs
    R = ROW_BLOCK
    t_idx = pl.program_id(2)
    dot = functools.partial(jnp.dot, preferred_element_type=F32)
    dn = (((1,), (1,)), ((), ()))
    dn0 = (((0,), (0,)), ((), ()))

    if nseq == 1:
        @pl.when(t_idx == 0)
        def _():
            if has_state:
                s_scr[...] = s0_ref[...].reshape(s_scr.shape)
            else:
                s_scr[...] = jnp.zeros_like(s_scr)

    seq_of_row = lax.broadcasted_iota(jnp.int32, (R, 1), 0) // tc
    gn = gn_ref[...]

    def local_part(rb):
        sl = pl.ds(rb * R, R)
        q = q_ref[sl, :].astype(F32)
        kk = k_ref[sl, :].astype(F32)
        vb = v_ref[sl, :]
        b = b_ref[sl, :]
        scores = lax.dot_general(q_ref[sl, :], k_ref[sl, :], dn, preferred_element_type=F32) * msk_ref[0]
        m, level = 1, 1
        while m < min(tc, 8):
            w = jnp.exp(-jnp.abs(b - _anchor(b, m)))
            s_m = lax.dot_general((q * w).astype(BF16), (kk * w).astype(BF16), dn, preferred_element_type=F32)
            scores = scores + s_m * msk_ref[level]
            m *= 2
            level += 1
        while m < tc:
            nblk = R // (2 * m)
            q_parts, k_parts, spans = [], [], []
            for blk in range(nblk):
                lo, mid, hi = blk * 2 * m, blk * 2 * m + m, (blk + 1) * 2 * m
                anchor = b[mid - 1:mid, :]
                q_parts.append(q[mid:hi] * jnp.exp(b[mid:hi] - anchor))
                k_parts += [kk[lo:mid] * jnp.exp(anchor - b[lo:mid]), jnp.zeros((m, HEAD_DIM), F32)]
                spans.append((lo, mid, hi))
            s_m = lax.dot_general(jnp.concatenate(q_parts, axis=0).astype(BF16),
                                  jnp.concatenate(k_parts, axis=0).astype(BF16), dn, preferred_element_type=F32)
            rows = []
            for blk, (lo, mid, hi) in enumerate(spans):
                piece = s_m[blk * m:(blk + 1) * m]
                if nblk > 1:
                    piece = piece * msk_ref[level, mid:hi, :]
                rows += [scores[lo:mid], scores[mid:hi] + piece]
            scores = jnp.concatenate(rows, axis=0)
            m *= 2
            level += 1
        e_end = _block_end(b, tc)
        qe = q * jnp.exp(b)
        ke = kk * jnp.exp(e_end - b)
        if nseq > 1:
            qe = jnp.concatenate([jnp.where(seq_of_row == i, qe, 0.0) for i in range(nseq)], axis=1)
            ke = jnp.concatenate([jnp.where(seq_of_row == i, ke, 0.0) for i in range(nseq)], axis=1)
        ds = lax.dot_general(ke.astype(BF16), vb, dn0, preferred_element_type=F32)
        decs = []
        for i in range(nseq):
            dec_row = jnp.exp(e_end[i * tc:i * tc + 1, :])
            decs.append(jnp.transpose(jnp.broadcast_to(dec_row, (HEAD_DIM, HEAD_DIM))))
        dec = decs[0] if nseq == 1 else jnp.concatenate(decs, axis=0)
        return jnp.concatenate([scores.astype(BF16), qe.astype(BF16)], axis=1), vb, ds, dec

    parts = [local_part(rb) for rb in range(nrb)]

    s_run = s_scr[...] if nseq == 1 else None
    for rb in range(nrb):
        sq, vb, ds, dec = parts[rb]
        sl = pl.ds(rb * R, R)
        if nseq == 1:
            s_in = s_run
        else:
            s_in = s0_ref[rb * nseq:(rb + 1) * nseq].reshape(nseq * HEAD_DIM, HEAD_DIM)
        o = dot(sq, jnp.concatenate([vb, s_in.astype(BF16)], axis=0))
        s_new = dec * s_in + ds
        on = o * lax.rsqrt(jnp.mean(o * o, axis=-1, keepdims=True) + EPS) * gn
        o_ref[sl, :] = (on * sg_ref[sl, :].astype(F32)).astype(o_ref.dtype)
        if nseq == 1:
            s_run = s_new
        else:
            st_ref[n_prev, rb * nseq:(rb + 1) * nseq] = s_new.reshape(nseq, HEAD_DIM, HEAD_DIM)

    if nseq == 1:
        s_scr[...] = s_run

        @pl.when(t_idx == nt - 1)
        def _():
            st_ref[n_prev] = s_run.reshape(st_ref.shape[1:])
            if n_prev:
                st_ref[0:n_prev] = prev_ref[...]
    elif n_prev:
        st_ref[0:n_prev] = prev_ref[...]


def _level_consts(tc):
    R = ROW_BLOCK
    t = np.arange(R)[:, None]
    s = np.arange(R)[None, :]
    cum = ((s <= t) & (t // tc == s // tc)).astype(np.float32)
    x = t ^ s
    masks = [t == s]
    m = 1
    while m < tc:
        masks.append((t > s) & (x >= m) & (x < 2 * m))
        m *= 2
    return jnp.asarray(cum, BF16), jnp.asarray(np.stack(masks), F32)


def _hgrn_scan(q, k, v, b, sg, gnorm, masks, state, layer_j, prev, *, bsz, seq):
    nh, m, _ = q.shape
    R = ROW_BLOCK
    has_state = state is not None
    n_prev = 0 if prev is None else prev.shape[0]
    if seq >= R:
        tc, nseq = R, 1
        tt = min(HGRN_TT, seq)
        nt = seq // tt
        nb = 1
        grid = (nh, bsz, nt)
        row_map = lambda h, b, t: (h, b * nt + t, 0)
    else:
        assert has_state
        tc, nseq = seq, R // seq
        tt = min(HGRN_TT_SHORT, m)
        nt = 1
        nb = tt // seq
        grid = (nh, m // tt, 1)
        row_map = lambda h, b, t: (h, b, 0)
    rspec = pl.BlockSpec((None, tt, HEAD_DIM), row_map)
    const2 = lambda h, b, t: (0, 0)
    in_specs = [rspec] * 5 + [pl.BlockSpec((1, HEAD_DIM), const2), pl.BlockSpec(masks.shape, lambda h, b, t: (0, 0, 0))]
    args = [q, k, v, b, sg, gnorm, masks]
    if has_state:
        in_specs.append(pl.BlockSpec((None, nb, None, HEAD_DIM, HEAD_DIM), lambda h, b, t: (layer_j, b, h, 0, 0)))
        args.append(state)
    st_map = lambda h, b, t: (0, b, h, 0, 0)
    if n_prev:
        in_specs.append(pl.BlockSpec((n_prev, nb, None, HEAD_DIM, HEAD_DIM), st_map))
        args.append(prev)
    return pl.pallas_call(
        functools.partial(_hgrn_scan_kernel, tc=tc, nseq=nseq, nrb=tt // R, nt=nt, has_state=has_state,
                          n_prev=n_prev),
        grid=grid,
        in_specs=in_specs,
        out_specs=[rspec, pl.BlockSpec((n_prev + 1, nb, None, HEAD_DIM, HEAD_DIM), st_map)],
        out_shape=[jax.ShapeDtypeStruct((nh, m, HEAD_DIM), BF16),
                   jax.ShapeDtypeStruct((n_prev + 1, bsz, nh, HEAD_DIM, HEAD_DIM), F32)],
        scratch_shapes=[pltpu.VMEM((HEAD_DIM, HEAD_DIM), F32)],
        compiler_params=_cparams(("parallel", "parallel", "arbitrary")),
        name="hgrn_scan",
    )(*args)


def _hgrn_out_kernel(x_ref, og_ref, w_ref, out_ref, *, nh):
    og = jnp.concatenate([og_ref[hh] for hh in range(nh)], axis=-1)
    out_ref[...] = x_ref[...] + jnp.dot(og, w_ref[...], preferred_element_type=F32)


def _hgrn_out(x, og, w_out, layer_j):
    m, d = x.shape
    nh = og.shape[0]
    tm = min(1024, m)
    return pl.pallas_call(
        functools.partial(_hgrn_out_kernel, nh=nh),
        grid=(m // tm,),
        in_specs=[
            pl.BlockSpec((tm, d), lambda i: (i, 0)),
            pl.BlockSpec((nh, tm, HEAD_DIM), lambda i: (0, i, 0)),
            pl.BlockSpec((None, d, d), lambda i: (layer_j, 0, 0)),
        ],
        out_specs=pl.BlockSpec((tm, d), lambda i: (i, 0)),
        out_shape=jax.ShapeDtypeStruct((m, d), F32),
        compiler_params=_cparams(("parallel",)),
        name="hgrn_out",
    )(x, og, w_out)


def _split3(a):
    hi = a.astype(BF16)
    r1 = a - hi.astype(F32)
    mid = r1.astype(BF16)
    lo = (r1 - mid.astype(F32)).astype(BF16)
    return hi, mid, lo


def _dot_precise(a, b):
    ah, am, al = _split3(a)
    bh, bm, bl = _split3(b)
    d = functools.partial(jnp.dot, preferred_element_type=F32)
    return (d(ah, bh) + (d(ah, bm) + d(am, bh))) + ((d(am, bm) + d(ah, bl)) + d(al, bh))


def _s5_prep_kernel(arc_ref, aic_ref, arr_ref, air_ref, ldt_ref, x1_ref, x2_ref, y1_ref, y2_ref, plt_ref, pls_ref,
                    w1_ref, w2_ref, w3_ref, pw_ref):
    L, GC, P, GL = S5_L, S5_GC, S5_P, S5_GL
    W = L * GC
    TW = L * GL * GC
    lane_t = lax.broadcasted_iota(jnp.int32, (1, W), 1) // GC
    row_s = lax.broadcasted_iota(jnp.int32, (W, 1), 0) // GC
    sgn_l = jnp.where(lax.broadcasted_iota(jnp.int32, (1, 2 * P), 1) < P, -1.0, 1.0)
    sgn_s = jnp.where(lax.broadcasted_iota(jnp.int32, (2 * P, 1), 0) < P, 1.0, -1.0)
    row16 = lax.broadcasted_iota(jnp.int32, (16, 1), 0)
    dot = functools.partial(jnp.dot, preferred_element_type=F32)

    def cmul(ar, ai, br, bi):
        return ar * br - ai * bi, ar * bi + ai * br

    bases = []
    pw = jnp.zeros((16, GL * 2 * P), F32)
    for g in range(GL):
        dt = jnp.exp(ldt_ref[g])
        a_re_c, a_im_c = arc_ref[g], aic_ref[g]
        er = jnp.exp(a_re_c * dt)
        lbr_c, lbi_c = er * jnp.cos(a_im_c * dt), er * jnp.sin(a_im_c * dt)
        a_re_r, a_im_r = arr_ref[g], air_ref[g]
        er_r = jnp.exp(a_re_r * dt)
        lbr_r, lbi_r = er_r * jnp.cos(a_im_r * dt), er_r * jnp.sin(a_im_r * dt)
        den = a_re_r * a_re_r + a_im_r * a_im_r
        nr, ni = lbr_r - 1.0, lbi_r
        f_r = (nr * a_re_r + ni * a_im_r) / den
        f_i = (ni * a_re_r - nr * a_im_r) / den

        pc = [(jnp.ones_like(lbr_c), jnp.zeros_like(lbr_c))]
        pr = [(jnp.ones_like(lbr_r), jnp.zeros_like(lbr_r))]
        for _ in range(L):
            pc.append(cmul(pc[-1][0], pc[-1][1], lbr_c, lbi_c))
            pr.append(cmul(pr[-1][0], pr[-1][1], lbr_r, lbi_r))

        q_re = jnp.zeros((2 * P, W), F32)
        q_im = jnp.zeros((2 * P, W), F32)
        for t in range(L):
            sel = lane_t == t
            q_re = jnp.where(sel, pc[t][0], q_re)
            q_im = jnp.where(sel, pc[t][1], q_im)
        y1, y2 = y1_ref[g], y2_ref[g]
        mp = sgn_s * (q_re * y1) - q_im * y2
        q1_re, q1_im = cmul(q_re, q_im, lbr_c, lbi_c)
        w3g = (sgn_s * (q1_re * y1) - q1_im * y2).astype(BF16)
        z3 = dot(w3g, plt_ref[g]).astype(BF16)
        for ri in range(2):
            w3_ref[ri * GL * P + g * P:ri * GL * P + (g + 1) * P, :] = z3[ri * P:(ri + 1) * P]

        cf_re = jnp.zeros((W, 2 * P), F32)
        cf_im = jnp.zeros((W, 2 * P), F32)
        for s in range(L):
            sel = row_s == s
            cf_re = jnp.where(sel, pr[L - 1 - s][0], cf_re)
            cf_im = jnp.where(sel, pr[L - 1 - s][1], cf_im)
        cf_re, cf_im = cmul(cf_re, cf_im, f_r, f_i)
        x1, x2 = x1_ref[g], x2_ref[g]
        w2g = (cf_re * x1 + (cf_im * sgn_l) * x2).astype(BF16)
        v2 = dot(w2g, pls_ref[g]).astype(BF16)
        for s in range(L):
            w2_ref[s * GL * GC + g * GC:s * GL * GC + (g + 1) * GC, :] = v2[s * GC:(s + 1) * GC]

        bbp = f_r * x1[0:GC] + (f_i * sgn_l) * x2[0:GC]
        base = _dot_precise(bbp, mp).astype(BF16)
        bases.append(dot(base, plt_ref[g]))

        tab = jnp.zeros((16, 2 * P), F32)
        cur = pr[L]
        for k in range(8):
            tab = jnp.where(row16 == k, cur[0], tab)
            tab = jnp.where(row16 == 8 + k, cur[1] * sgn_l, tab)
            cur = cmul(cur[0], cur[1], cur[0], cur[1])
        th, tm_, tl = _split3(tab)
        pls = pls_ref[g]
        pw = pw + ((dot(th, pls) + dot(tm_, pls)) + dot(tl, pls))

    base8 = jnp.concatenate(bases, axis=0)
    lane = lax.broadcasted_iota(jnp.int32, (1, TW), 1)
    blk = GL * GC
    w1_ref[0:blk, :] = base8.astype(BF16)
    for s in range(1, L):
        w1_ref[s * blk:(s + 1) * blk, :] = jnp.where(lane >= s * blk, pltpu.roll(base8, s * blk, 1), 0.0).astype(BF16)
    pw_ref[...] = pw


def _s5_placements():
    L, GC, P, GL = S5_L, S5_GC, S5_P, S5_GL
    plt = np.zeros((GL, L * GC, L * GL * GC), np.float32)
    pls = np.zeros((GL, 2 * P, 2 * GL * P), np.float32)
    for gl in range(GL):
        for t in range(L):
            for c in range(GC):
                plt[gl, t * GC + c, t * GL * GC + gl * GC + c] = 1.0
        for ri in range(2):
            for p in range(P):
                pls[gl, ri * P + p, ri * GL * P + gl * P + p] = 1.0
    return jnp.asarray(plt, BF16), jnp.asarray(pls, BF16)


def _s5_prep(a_re, a_im, log_dt, b_re, b_im, c_re, c_im):
    g, p = a_re.shape
    L, GC, GL = S5_L, S5_GC, S5_GL
    W = L * GC
    TW = L * GL * GC
    SW = 2 * GL * p
    nblk = g // GL
    a_re2 = jnp.concatenate([a_re, a_re], axis=1)
    a_im2 = jnp.concatenate([a_im, a_im], axis=1)
    arc, aic = a_re2[:, :, None], a_im2[:, :, None]
    arr, air = a_re2[:, None, :], a_im2[:, None, :]
    ldt = log_dt[:, None, None]
    btr, bti = jnp.transpose(b_re, (0, 2, 1)), jnp.transpose(b_im, (0, 2, 1))
    x1 = jnp.tile(jnp.concatenate([btr, bti], axis=2), (1, L, 1))
    x2 = jnp.tile(jnp.concatenate([bti, btr], axis=2), (1, L, 1))
    ctr, cti = jnp.transpose(c_re, (0, 2, 1)), jnp.transpose(c_im, (0, 2, 1))
    y1 = jnp.tile(jnp.concatenate([ctr, cti], axis=1), (1, 1, L))
    y2 = jnp.tile(jnp.concatenate([cti, ctr], axis=1), (1, 1, L))
    plt, pls = _s5_placements()

    def spec(shape):
        return pl.BlockSpec((GL,) + shape, lambda i: (i, 0, 0))

    const3 = lambda i: (0, 0, 0)
    out3 = lambda i: (i, 0, 0)
    return pl.pallas_call(
        _s5_prep_kernel,
        grid=(nblk,),
        in_specs=[spec((2 * p, 1)), spec((2 * p, 1)), spec((1, 2 * p)), spec((1, 2 * p)), spec((1, 1)),
                  spec((W, 2 * p)), spec((W, 2 * p)), spec((2 * p, W)), spec((2 * p, W)),
                  pl.BlockSpec(plt.shape, const3), pl.BlockSpec(pls.shape, const3)],
        out_specs=[pl.BlockSpec((None, TW, TW), out3), pl.BlockSpec((None, TW, SW), out3),
                   pl.BlockSpec((None, SW, TW), out3), pl.BlockSpec((None, 16, SW), out3)],
        out_shape=[jax.ShapeDtypeStruct((nblk, TW, TW), BF16), jax.ShapeDtypeStruct((nblk, TW, SW), BF16),
                   jax.ShapeDtypeStruct((nblk, SW, TW), BF16), jax.ShapeDtypeStruct((nblk, 16, SW), F32)],
        compiler_params=_cparams(("parallel",)),
        name="s5_prep",
    )(arc, aic, arr, air, ldt, x1, x2, y1, y2, plt, pls)


def _s5_core_kernel(*refs, nrows, has_state):
    if has_state:
        hn_ref, w1_ref, w2_ref, w3_ref, pw_ref, s0_ref, y_ref, fin_ref = refs
    else:
        hn_ref, w1_ref, w2_ref, w3_ref, pw_ref, y_ref, fin_ref, x_scr, p_scr, s1_scr, sin_scr = refs
    L = S5_L
    half = S5_GL * S5_P
    dot = functools.partial(jnp.dot, preferred_element_type=F32)

    ns = half // 128
    strips = lambda a: [a[:, s * 128:(s + 1) * 128] for s in range(2 * ns)]
    join = lambda parts: jnp.concatenate(parts, axis=1)

    def scale_add(a, k, c):
        out = [None] * (2 * ns)
        for st in range(ns):
            wr = pw_ref[k:k + 1, st * 128:(st + 1) * 128]
            wi = pw_ref[8 + k:9 + k, half + st * 128:half + (st + 1) * 128]
            ar, ai = a[st], a[ns + st]
            out[st] = c[st] + (ar * wr - ai * wi)
            out[ns + st] = c[ns + st] + (ar * wi + ai * wr)
        return out

    u = jnp.concatenate([hn_ref[pl.ds(s, nrows, stride=L), :] for s in range(L)], axis=1).astype(BF16)
    x = dot(u, w2_ref[...])
    y_local = dot(u, w1_ref[...])
    if has_state:
        s_in = s0_ref[...]
        fin_ref[...] = join(scale_add(strips(s_in), 0, strips(x)))
    else:
        n1, n2 = nrows // 2, nrows // 4
        every = range(2 * ns)

        def put(ref, parts, start=None, n=None):
            for s in every:
                if start is None:
                    ref[s] = parts[s]
                else:
                    ref[s, pl.ds(start, n, stride=2), :] = parts[s]

        def alternate(ref, start, n):
            return [ref[s, pl.ds(start, n, stride=2), :] for s in every]

        put(x_scr, strips(x))
        e0, o0 = alternate(x_scr, 0, n1), alternate(x_scr, 1, n1)
        put(p_scr, scale_add(e0, 0, o0))
        e1, o1 = alternate(p_scr, 0, n2), alternate(p_scr, 1, n2)
        z = scale_add(e1, 1, o1)
        row = lax.broadcasted_iota(jnp.int32, (n2, 1), 0)

        def shift(parts, sh):
            if sh % 8 == 0:
                return [jnp.concatenate([jnp.zeros((sh, 128), F32), a[:n2 - sh]], axis=0) for a in parts]
            return [jnp.where(row >= sh, pltpu.roll(a, sh, 0), 0.0) for a in parts]

        sh, k = 1, 2
        while sh < n2:
            z = scale_add(shift(z, sh), k, z)
            sh *= 2
            k += 1
        fin_ref[...] = join([a[n2 - 1:n2, :] for a in z])
        s2 = shift(z, 1)
        put(s1_scr, s2, 0, n2)
        put(s1_scr, scale_add(s2, 1, e1), 1, n2)
        s1 = [s1_scr[s] for s in every]
        put(sin_scr, s1, 0, n1)
        put(sin_scr, scale_add(s1, 0, e0), 1, n1)
        s_in = join([sin_scr[s] for s in every])
    y = y_local + dot(s_in.astype(BF16), w3_ref[...])
    lanes = y_ref.shape[1]
    for t in range(L):
        y_ref[pl.ds(t, nrows, stride=L), :] = y[:, t * lanes:(t + 1) * lanes]


def _s5_core(hn, w1, w2, w3, pw, s0, *, bsz, seq):
    m, d = hn.shape
    L = S5_L
    nblk, tw, sw = w2.shape
    lanes = d // nblk
    has_state = s0 is not None
    assert has_state == (seq == L) and seq % L == 0 and seq // L <= 256
    if has_state:
        tile, nb = m, 1
        fin_shape = jax.ShapeDtypeStruct((bsz, nblk * sw), F32)
        fin_spec = pl.BlockSpec((bsz, sw), lambda g, b: (0, g))
    else:
        tile, nb = seq, bsz
        fin_shape = jax.ShapeDtypeStruct((bsz, 1, nblk * sw), F32)
        fin_spec = pl.BlockSpec((None, 1, sw), lambda g, b: (b, 0, g))
    nrows = tile // L
    wmap = lambda g, b: (g, 0, 0)
    in_specs = [
        pl.BlockSpec((tile, lanes), lambda g, b: (b, g)),
        pl.BlockSpec((None, tw, tw), wmap), pl.BlockSpec((None, tw, sw), wmap),
        pl.BlockSpec((None, sw, tw), wmap), pl.BlockSpec((None, 16, sw), wmap),
    ]
    args = [hn, w1, w2, w3, pw]
    if has_state:
        in_specs.append(pl.BlockSpec((bsz, sw), lambda g, b: (0, g)))
        args.append(s0)
    return pl.pallas_call(
        functools.partial(_s5_core_kernel, nrows=nrows, has_state=has_state),
        grid=(nblk, nb),
        in_specs=in_specs,
        out_specs=[pl.BlockSpec((tile, lanes), lambda g, b: (b, g)), fin_spec],
        out_shape=[jax.ShapeDtypeStruct((m, d), F32), fin_shape],
        scratch_shapes=[] if has_state else [pltpu.VMEM((sw // 128, nrows // r, 128), F32) for r in (1, 2, 2, 1)],
        compiler_params=_cparams(("parallel", "parallel")),
        name="s5_core",
    )(*args)


def _gelu_tanh(x):
    return 0.5 * x * (1.0 + jnp.tanh(0.7978845608028654 * (x + 0.044715 * (x * x * x))))


def _s5_out_kernel(x_ref, ln_ref, y_ref, d_ref, w_ref, out_ref, *, d):
    x = x_ref[...]
    h = _rms(x, ln_ref[...])
    y = y_ref[...].astype(F32) + d_ref[...] * h
    z = _gelu_tanh(y).astype(BF16)
    ag = jnp.dot(z, w_ref[...], preferred_element_type=F32)
    out_ref[...] = x + ag[:, :d] * _sigmoid(ag[:, d:])


def _s5_out(x, ln, y, d_skip, w_glu, layer_j):
    m, d = x.shape
    tm = min(1024, m)
    return pl.pallas_call(
        functools.partial(_s5_out_kernel, d=d),
        grid=(m // tm,),
        in_specs=[
            pl.BlockSpec((tm, d), lambda i: (i, 0)),
            pl.BlockSpec((1, d), lambda i: (0, 0)),
            pl.BlockSpec((tm, d), lambda i: (i, 0)),
            pl.BlockSpec((1, d), lambda i: (0, 0)),
            pl.BlockSpec((None, d, 2 * d), lambda i: (layer_j, 0, 0)),
        ],
        out_specs=pl.BlockSpec((tm, d), lambda i: (i, 0)),
        out_shape=jax.ShapeDtypeStruct((m, d), F32),
        compiler_params=_cparams(("parallel",)),
        name="s5_out",
    )(x, ln, y, d_skip, w_glu)


def _unpack_state(fin, bsz):
    f = fin.reshape(bsz, -1, 2, S5_GL, S5_P)
    return f[:, :, 0].reshape(bsz, -1, S5_P), f[:, :, 1].reshape(bsz, -1, S5_P)


def _layer(i, st, wts, w_bf):
    j = i // 2
    depth = wts["ln_ffn1"].shape[0]
    x, bsz, seq = st["x"], st["bsz"], st["seq"]
    is_hgrn = i % 2 == 0

    def ffn(name, x, ln, gain2, **kw):
        key = (name, i)
        kw = dict(dict(hn_dtype=None, final_norm=False), **kw)
        if key in w_bf:
            w_in, w_out = w_bf[key]
            out, hn, _ = _ffn_resident(x, ln, w_in, w_out, gain2, **kw)
        else:
            out, hn, w_bf[key] = _ffn_cast(x, ln, wts[name + "_w_in"], wts[name + "_w_out"], i, gain2, **kw)
        return out, hn

    x, hn = ffn("ffn1", x, wts["ln_ffn1"][i:i + 1], wts["ln_mix"][i:i + 1], hn_dtype=BF16 if is_hgrn else F32)
    if is_hgrn:
        cum, masks = _level_consts(min(seq, ROW_BLOCK))
        q, k, v, b, sg = _hgrn_gates(hn, wts["hgrn_w_in"], j, wts["hgrn_lb_logits"], i, cum)
        og, st["new_hgrn"] = _hgrn_scan(q, k, v, b, sg, wts["hgrn_gnorm"][j:j + 1], masks, st["hgrn"], j,
                                        st["new_hgrn"], bsz=bsz, seq=seq)
        x = _hgrn_out(x, og, wts["hgrn_w_out"], j)
    else:
        w1, w2, w3, pw = wts["s5_mats"][j]
        s0 = None
        if st["re"] is not None:
            pack = lambda a: a[j].reshape(bsz, -1, 1, S5_GL, S5_P)
            s0 = jnp.concatenate([pack(st["re"]), pack(st["im"])], axis=2).reshape(bsz, -1)
        y, fin = _s5_core(hn, w1, w2, w3, pw, s0, bsz=bsz, seq=seq)
        f_re, f_im = _unpack_state(fin, bsz)
        st["new_re"].append(f_re)
        st["new_im"].append(f_im)
        x = _s5_out(x, wts["ln_mix"][i:i + 1], y, wts["s5_d"][j:j + 1], wts["s5_w_glu"], j)
    last = i == depth - 1
    gain2 = wts["ln_final"] if last else wts["ln_ffn2"][i:i + 1]
    x, _ = ffn("ffn2", x, wts["ln_ffn2"][i:i + 1], gain2, final_norm=last)
    st["x"] = x


def kernel(x_prompt, x_sample, state_hgrn, state_s5_re, state_s5_im, ln_ffn1, ffn1_w_in, ffn1_w_out, ln_mix, ln_ffn2, ffn2_w_in, ffn2_w_out, hgrn_lb_logits, hgrn_w_in, hgrn_gnorm, hgrn_w_out, s5_a_re, s5_a_im, s5_log_dt, s5_b_re, s5_b_im, s5_c_re, s5_c_im, s5_d, s5_w_glu, ln_final):
    depth, d = ln_ffn1.shape
    wts = dict(
        ln_ffn1=ln_ffn1, ln_mix=ln_mix, ln_ffn2=ln_ffn2, ln_final=ln_final.reshape(1, -1),
        ffn1_w_in=ffn1_w_in, ffn1_w_out=ffn1_w_out, ffn2_w_in=ffn2_w_in, ffn2_w_out=ffn2_w_out,
        hgrn_lb_logits=hgrn_lb_logits, hgrn_w_in=hgrn_w_in.astype(BF16), hgrn_gnorm=hgrn_gnorm,
        hgrn_w_out=hgrn_w_out.astype(BF16), s5_d=s5_d, s5_w_glu=s5_w_glu.astype(BF16),
        s5_mats=[_s5_prep(s5_a_re[j], s5_a_im[j], s5_log_dt[j], s5_b_re[j], s5_b_im[j], s5_c_re[j], s5_c_im[j])
                 for j in range(s5_a_re.shape[0])],
    )

    def stream(x3, hgrn, re, im):
        bsz, seq, _ = x3.shape
        return dict(x=x3.reshape(bsz * seq, d), bsz=bsz, seq=seq, hgrn=hgrn, re=re, im=im,
                    new_hgrn=None, new_re=[], new_im=[])

    sample = stream(x_sample, state_hgrn, state_s5_re, state_s5_im)
    prompt = stream(x_prompt, None, None, None)
    w_bf = {}
    for i in range(depth):
        _layer(i, sample, wts, w_bf)
        _layer(i, prompt, wts, w_bf)

    def outs(st, x3):
        return (st["x"].reshape(x3.shape), st["new_hgrn"], jnp.stack(st["new_re"]), jnp.stack(st["new_im"]))

    y_p, hg_p, re_p, im_p = outs(prompt, x_prompt)
    y_s, hg_s, re_s, im_s = outs(sample, x_sample)
    return (y_p, y_s, hg_p, re_p, im_p, hg_s, re_s, im_s)
```

```python
import functools

import numpy as np
import jax
import jax.numpy as jnp
from jax import lax
from jax.experimental import pallas as pl
from jax.experimental.pallas import tpu as pltpu

F32 = jnp.float32
BF16 = jnp.bfloat16

EPS = 1e-6
GATE_FLOOR = 1e-30
MACARON_SCALE = 0.5
LB_CLIP = 1.0 - 1e-4

HEAD_DIM = 128
S5_GC = 16
S5_P = 64
S5_L = 8
S5_GL = 8
ROW_BLOCK = 128
FFN_TM = 1024
FFN_TF = 256
HGRN_TT = 2048
HGRN_GATES_TM = 1024
HGRN_TT_SHORT = 512
VMEM_LIMIT = 52 * 1024 * 1024


def _cparams(sem):
    return pltpu.CompilerParams(dimension_semantics=sem, vmem_limit_bytes=VMEM_LIMIT)


def _rms(x, gain):
    return x * lax.rsqrt(jnp.mean(x * x, axis=-1, keepdims=True) + EPS) * gain


def _sigmoid(x):
    return 1.0 / (1.0 + jnp.exp(-x))


def _ffn_cast_kernel(x_ref, ln_ref, wa_ref, wb_ref, wout_ref, g2_ref, *rest, nf, hn_dtype, final_norm):
    rest = list(rest)
    out_ref = rest.pop(0)
    hn_ref = rest.pop(0) if hn_dtype is not None else None
    wa_bf_ref, wb_bf_ref, wout_bf_ref, h_scr, acc_scr = rest
    j = pl.program_id(1)

    @pl.when(j == 0)
    def _():
        h_scr[...] = _rms(x_ref[...], ln_ref[...]).astype(BF16)
        acc_scr[...] = jnp.zeros_like(acc_scr)

    wa, wb, wo = wa_ref[...].astype(BF16), wb_ref[...].astype(BF16), wout_ref[...].astype(BF16)
    wa_bf_ref[...] = wa
    wb_bf_ref[...] = wb
    wout_bf_ref[...] = wo
    h = h_scr[...]
    a = jnp.dot(h, wa, preferred_element_type=F32)
    b = jnp.dot(h, wb, preferred_element_type=F32)
    g = (a * _sigmoid(a) * b).astype(BF16)
    acc_scr[...] += jnp.dot(g, wo, preferred_element_type=F32)

    @pl.when(j == nf - 1)
    def _():
        y = x_ref[...] + MACARON_SCALE * acc_scr[...]
        if final_norm:
            out_ref[...] = _rms(y, g2_ref[...])
        else:
            out_ref[...] = y
        if hn_ref is not None:
            hn_ref[...] = _rms(y, g2_ref[...]).astype(hn_ref.dtype)


def _ffn_resident_kernel(x_ref, ln_ref, wa_ref, wb_ref, wout_ref, g2_ref, *rest, tf, hn_dtype, final_norm, mix_heads):
    rest = list(rest)
    if mix_heads:
        og_ref, wmix_ref = rest[:2]
        del rest[:2]
    out_ref = rest.pop(0)
    hn_ref = rest.pop(0) if hn_dtype is not None else None
    g_scr, = rest
    x = x_ref[...]
    if mix_heads:
        og = jnp.concatenate([og_ref[hh] for hh in range(mix_heads)], axis=-1)
        x = x + jnp.dot(og, wmix_ref[...], preferred_element_type=F32)
    h = _rms(x, ln_ref[...]).astype(BF16)
    for j in range(g_scr.shape[1] // tf):
        cols = slice(j * tf, (j + 1) * tf)
        a = jnp.dot(h, wa_ref[:, cols], preferred_element_type=F32)
        b = jnp.dot(h, wb_ref[:, cols], preferred_element_type=F32)
        g_scr[:, cols] = (a * _sigmoid(a) * b).astype(BF16)
    y = x + MACARON_SCALE * jnp.dot(g_scr[...], wout_ref[...], preferred_element_type=F32)
    if final_norm:
        out_ref[...] = _rms(y, g2_ref[...])
    else:
        out_ref[...] = y
    if hn_ref is not None:
        hn_ref[...] = _rms(y, g2_ref[...]).astype(hn_ref.dtype)


def _ffn_resident(x, ln, w_in, w_out, gain2, *, hn_dtype, final_norm, mix=None):
    m, d = x.shape
    wa, wb = w_in
    f = w_out.shape[1]
    tm = min(FFN_TM, m)
    row = lambda i: (i, 0)
    once = pl.Buffered(1)
    out_shape = [jax.ShapeDtypeStruct((m, d), F32)]
    out_specs = [pl.BlockSpec((tm, d), row)]
    if hn_dtype is not None:
        out_shape.append(jax.ShapeDtypeStruct((m, d), hn_dtype))
        out_specs.append(pl.BlockSpec((tm, d), row))
    in_specs = [
        pl.BlockSpec((tm, d), row),
        pl.BlockSpec((1, d), lambda i: (0, 0)),
        pl.BlockSpec((None, d, f), lambda i: (0, 0, 0), pipeline_mode=once),
        pl.BlockSpec((None, d, f), lambda i: (0, 0, 0), pipeline_mode=once),
        pl.BlockSpec((None, f, d), lambda i: (0, 0, 0), pipeline_mode=once),
        pl.BlockSpec((1, d), lambda i: (0, 0)),
    ]
    args = [x, ln, wa, wb, w_out, gain2]
    mix_heads = 0
    if mix is not None:
        og, w_mix, mix_layer = mix
        mix_heads = og.shape[0]
        in_specs += [pl.BlockSpec((mix_heads, tm, og.shape[2]), lambda i: (0, i, 0)),
                     pl.BlockSpec((None, d, d), lambda i: (mix_layer, 0, 0), pipeline_mode=once)]
        args += [og, w_mix]
    res = pl.pallas_call(
        functools.partial(_ffn_resident_kernel, tf=FFN_TF, hn_dtype=hn_dtype, final_norm=final_norm,
                          mix_heads=mix_heads),
        grid=(m // tm,),
        in_specs=in_specs,
        out_specs=out_specs,
        out_shape=out_shape,
        scratch_shapes=[pltpu.VMEM((tm, f), BF16)],
        compiler_params=_cparams(("parallel",)),
        name="ffn",
    )(*args)
    return res[0], (res[1] if hn_dtype is not None else None), None


def _ffn_cast(x, ln, w_in, w_out, layer, gain2, *, hn_dtype, final_norm):
    m, d = x.shape
    f = w_out.shape[1]
    tf = FFN_TF
    nf = f // tf
    assert m <= FFN_TM
    row = lambda i, j: (0, 0)
    out_shape = [jax.ShapeDtypeStruct((m, d), F32)]
    out_specs = [pl.BlockSpec((m, d), row)]
    if hn_dtype is not None:
        out_shape.append(jax.ShapeDtypeStruct((m, d), hn_dtype))
        out_specs.append(pl.BlockSpec((m, d), row))
    out_shape += [jax.ShapeDtypeStruct((1, d, f), BF16)] * 2 + [jax.ShapeDtypeStruct((1, f, d), BF16)]
    out_specs += [pl.BlockSpec((None, d, tf), lambda i, j: (0, 0, j))] * 2
    out_specs += [pl.BlockSpec((None, tf, d), lambda i, j: (0, j, 0))]
    res = pl.pallas_call(
        functools.partial(_ffn_cast_kernel, nf=nf, hn_dtype=hn_dtype, final_norm=final_norm),
        grid=(1, nf),
        in_specs=[
            pl.BlockSpec((m, d), row),
            pl.BlockSpec((1, d), row),
            pl.BlockSpec((None, d, tf), lambda i, j: (layer, 0, j)),
            pl.BlockSpec((None, d, tf), lambda i, j: (layer, 0, nf + j)),
            pl.BlockSpec((None, tf, d), lambda i, j: (layer, j, 0)),
            pl.BlockSpec((1, d), row),
        ],
        out_specs=out_specs,
        out_shape=out_shape,
        scratch_shapes=[pltpu.VMEM((m, d), BF16), pltpu.VMEM((m, d), F32)],
        compiler_params=_cparams(("arbitrary", "arbitrary")),
        name="ffn_cast",
    )(x, ln, w_in, w_in, w_out, gain2)
    res = list(res)
    out = res.pop(0)
    hn = res.pop(0) if hn_dtype is not None else None
    return out, hn, ((res[0], res[1]), res[2])


def _hgrn_gates_kernel(hn_ref, w_ref, lbl_ref, cum_ref, q_ref, k_ref, v_ref, b_ref, sg_ref, *, layer, d, nh):
    h = hn_ref[...]
    logits = lbl_ref[...]
    e = jnp.exp(logits - jnp.max(logits, axis=0, keepdims=True))
    p = e / jnp.sum(e, axis=0, keepdims=True)
    cum = p[0:1]
    for r in range(1, layer + 1):
        cum = cum + p[r:r + 1]
    lb = jnp.clip(cum - p[0:1], 0.0, LB_CLIP)

    def heads(ref, val):
        for hh in range(nh):
            ref[hh] = val[:, hh * HEAD_DIM:(hh + 1) * HEAD_DIM].astype(ref.dtype)

    pq = jnp.dot(h, w_ref[:, 0:d], preferred_element_type=F32)
    heads(q_ref, pq)
    pf = jnp.dot(h, w_ref[:, d:2 * d], preferred_element_type=F32)
    ez = jnp.exp(-jnp.abs(pf))
    r = 1.0 / (1.0 + ez)
    pos = pf >= 0.0
    sig_p = jnp.where(pos, r, ez * r)
    sig_n = jnp.where(pos, ez * r, r)
    fg = lb + (1.0 - lb) * sig_p
    hi, mid, lo = _split3(jnp.log(jnp.maximum(fg, GATE_FLOOR)))
    cum = cum_ref[...]
    dot = functools.partial(jnp.dot, preferred_element_type=F32)
    R = ROW_BLOCK
    for rb in range(h.shape[0] // R):
        rows = slice(rb * R, (rb + 1) * R)
        b = (dot(cum, hi[rows]) + dot(cum, mid[rows])) + dot(cum, lo[rows])
        for hh in range(nh):
            b_ref[hh, rows, :] = b[:, hh * HEAD_DIM:(hh + 1) * HEAD_DIM]
    heads(k_ref, (1.0 - lb) * sig_n)
    pv = jnp.dot(h, w_ref[:, 2 * d:3 * d], preferred_element_type=F32)
    heads(v_ref, pv)
    pg = jnp.dot(h, w_ref[:, 3 * d:4 * d], preferred_element_type=F32)
    heads(sg_ref, pg * _sigmoid(pg))


def _hgrn_gates(hn, w_in, layer_j, lb_logits, layer, cum):
    m, d = hn.shape
    nh = d // HEAD_DIM
    tm = min(HGRN_GATES_TM, m)
    hm = lambda dt: jax.ShapeDtypeStruct((nh, m, HEAD_DIM), dt)
    hspec = pl.BlockSpec((nh, tm, HEAD_DIM), lambda i: (0, i, 0))
    return pl.pallas_call(
        functools.partial(_hgrn_gates_kernel, layer=layer, d=d, nh=nh),
        grid=(m // tm,),
        in_specs=[
            pl.BlockSpec((tm, d), lambda i: (i, 0)),
            pl.BlockSpec((None, d, 4 * d), lambda i: (layer_j, 0, 0)),
            pl.BlockSpec(lb_logits.shape, lambda i: (0, 0)),
            pl.BlockSpec(cum.shape, lambda i: (0, 0)),
        ],
        out_specs=[hspec] * 5,
        out_shape=[hm(BF16), hm(BF16), hm(BF16), hm(F32), hm(BF16)],
        compiler_params=_cparams(("parallel",)),
        name="hgrn_gates",
    )(hn, w_in, lb_logits, cum)


def _anchor(b, m):
    R, n = b.shape
    if m >= 8:
        parts = [jnp.broadcast_to(b[blk * 2 * m + m - 1:blk * 2 * m + m, :], (2 * m, n)) for blk in range(R // (2 * m))]
        return parts[0] if len(parts) == 1 else jnp.concatenate(parts, axis=0)
    if m == 1:
        odd = jnp.bitwise_and(lax.broadcasted_iota(jnp.int32, (R, 1), 0), 1) == 1
        return jnp.where(odd, pltpu.roll(b, 1, 0), b)
    b3 = b.reshape(R // 8, 8, n)
    if m == 4:
        a3 = jnp.broadcast_to(b3[:, 3:4, :], b3.shape)
    else:
        sub = lax.broadcasted_iota(jnp.int32, (1, 8, 1), 1)
        a3 = jnp.where(sub < 4, jnp.broadcast_to(b3[:, 1:2, :], b3.shape), jnp.broadcast_to(b3[:, 5:6, :], b3.shape))
    return a3.reshape(R, n)


def _block_end(b, tc):
    R, n = b.shape
    if tc == 8:
        b3 = b.reshape(R // 8, 8, n)
        return jnp.broadcast_to(b3[:, 7:8, :], b3.shape).reshape(R, n)
    parts = [jnp.broadcast_to(b[blk * tc + tc - 1:blk * tc + tc, :], (tc, n)) for blk in range(R // tc)]
    return parts[0] if len(parts) == 1 else jnp.concatenate(parts, axis=0)


def _hgrn_scan_kernel(*refs, tc, nseq, nrb, nt, has_state, n_prev):
    refs = list(refs)
    q_ref, k_ref, v_ref, b_ref, sg_ref, gn_ref, msk_ref = refs[:7]
    del refs[:7]
    s0_ref = refs.pop(0) if has_state else None
    prev_ref = refs.pop(0) if n_prev else None
    o_ref, st_ref, s_scr = refs
    R = ROW_BLOCK
    t_idx = pl.program_id(2)
    dot = functools.partial(jnp.dot, preferred_element_type=F32)
    dn = (((1,), (1,)), ((), ()))
    dn0 = (((0,), (0,)), ((), ()))

    if nseq == 1:
        @pl.when(t_idx == 0)
        def _():
            if has_state:
                s_scr[...] = s0_ref[...].reshape(s_scr.shape)
            else:
                s_scr[...] = jnp.zeros_like(s_scr)

    seq_of_row = lax.broadcasted_iota(jnp.int32, (R, 1), 0) // tc
    gn = gn_ref[...]

    def local_part(rb):
        sl = pl.ds(rb * R, R)
        q = q_ref[sl, :].astype(F32)
        kk = k_ref[sl, :].astype(F32)
        vb = v_ref[sl, :]
        b = b_ref[sl, :]
        scores = lax.dot_general(q_ref[sl, :], k_ref[sl, :], dn, preferred_element_type=F32) * msk_ref[0]
        m, level = 1, 1
        while m < min(tc, 8):
            w = jnp.exp(-jnp.abs(b - _anchor(b, m)))
            s_m = lax.dot_general((q * w).astype(BF16), (kk * w).astype(BF16), dn, preferred_element_type=F32)
            scores = scores + s_m * msk_ref[level]
            m *= 2
            level += 1
        while m < tc:
            nblk = R // (2 * m)
            q_parts, k_parts, spans = [], [], []
            for blk in range(nblk):
                lo, mid, hi = blk * 2 * m, blk * 2 * m + m, (blk + 1) * 2 * m
                anchor = b[mid - 1:mid, :]
                q_parts.append(q[mid:hi] * jnp.exp(b[mid:hi] - anchor))
                k_parts += [kk[lo:mid] * jnp.exp(anchor - b[lo:mid]), jnp.zeros((m, HEAD_DIM), F32)]
                spans.append((lo, mid, hi))
            s_m = lax.dot_general(jnp.concatenate(q_parts, axis=0).astype(BF16),
                                  jnp.concatenate(k_parts, axis=0).astype(BF16), dn, preferred_element_type=F32)
            rows = []
            for blk, (lo, mid, hi) in enumerate(spans):
                piece = s_m[blk * m:(blk + 1) * m]
                if nblk > 1:
                    piece = piece * msk_ref[level, mid:hi, :]
                rows += [scores[lo:mid], scores[mid:hi] + piece]
            scores = jnp.concatenate(rows, axis=0)
            m *= 2
            level += 1
        e_end = _block_end(b, tc)
        qe = q * jnp.exp(b)
        ke = kk * jnp.exp(e_end - b)
        if nseq > 1:
            qe = jnp.concatenate([jnp.where(seq_of_row == i, qe, 0.0) for i in range(nseq)], axis=1)
            ke = jnp.concatenate([jnp.where(seq_of_row == i, ke, 0.0) for i in range(nseq)], axis=1)
        ds = lax.dot_general(ke.astype(BF16), vb, dn0, preferred_element_type=F32)
        decs = []
        for i in range(nseq):
            dec_row = jnp.exp(e_end[i * tc:i * tc + 1, :])
            decs.append(jnp.transpose(jnp.broadcast_to(dec_row, (HEAD_DIM, HEAD_DIM))))
        dec = decs[0] if nseq == 1 else jnp.concatenate(decs, axis=0)
        return jnp.concatenate([scores.astype(BF16), qe.astype(BF16)], axis=1), vb, ds, dec

    parts = [local_part(rb) for rb in range(nrb)]

    s_run = s_scr[...] if nseq == 1 else None
    for rb in range(nrb):
        sq, vb, ds, dec = parts[rb]
        sl = pl.ds(rb * R, R)
        if nseq == 1:
            s_in = s_run
        else:
            s_in = s0_ref[rb * nseq:(rb + 1) * nseq].reshape(nseq * HEAD_DIM, HEAD_DIM)
        o = dot(sq, jnp.concatenate([vb, s_in.astype(BF16)], axis=0))
        s_new = dec * s_in + ds
        on = o * lax.rsqrt(jnp.mean(o * o, axis=-1, keepdims=True) + EPS) * gn
        o_ref[sl, :] = (on * sg_ref[sl, :].astype(F32)).astype(o_ref.dtype)
        if nseq == 1:
            s_run = s_new
        else:
            st_ref[n_prev, rb * nseq:(rb + 1) * nseq] = s_new.reshape(nseq, HEAD_DIM, HEAD_DIM)

    if nseq == 1:
        s_scr[...] = s_run

        @pl.when(t_idx == nt - 1)
        def _():
            st_ref[n_prev] = s_run.reshape(st_ref.shape[1:])
            if n_prev:
                st_ref[0:n_prev] = prev_ref[...]
    elif n_prev:
        st_ref[0:n_prev] = prev_ref[...]


def _level_consts(tc):
    R = ROW_BLOCK
    t = np.arange(R)[:, None]
    s = np.arange(R)[None, :]
    cum = ((s <= t) & (t // tc == s // tc)).astype(np.float32)
    x = t ^ s
    masks = [t == s]
    m = 1
    while m < tc:
        masks.append((t > s) & (x >= m) & (x < 2 * m))
        m *= 2
    return jnp.asarray(cum, BF16), jnp.asarray(np.stack(masks), F32)


def _hgrn_scan(q, k, v, b, sg, gnorm, masks, state, layer_j, prev, *, bsz, seq):
    nh, m, _ = q.shape
    R = ROW_BLOCK
    has_state = state is not None
    n_prev = 0 if prev is None else prev.shape[0]
    if seq >= R:
        tc, nseq = R, 1
        tt = min(HGRN_TT, seq)
        nt = seq // tt
        nb = 1
        grid = (nh, bsz, nt)
        row_map = lambda h, b, t: (h, b * nt + t, 0)
    else:
        assert has_state
        tc, nseq = seq, R // seq
        tt = min(HGRN_TT_SHORT, m)
        nt = 1
        nb = tt // seq
        grid = (nh, m // tt, 1)
        row_map = lambda h, b, t: (h, b, 0)
    rspec = pl.BlockSpec((None, tt, HEAD_DIM), row_map)
    const2 = lambda h, b, t: (0, 0)
    in_specs = [rspec] * 5 + [pl.BlockSpec((1, HEAD_DIM), const2), pl.BlockSpec(masks.shape, lambda h, b, t: (0, 0, 0))]
    args = [q, k, v, b, sg, gnorm, masks]
    if has_state:
        in_specs.append(pl.BlockSpec((None, nb, None, HEAD_DIM, HEAD_DIM), lambda h, b, t: (layer_j, b, h, 0, 0)))
        args.append(state)
    st_map = lambda h, b, t: (0, b, h, 0, 0)
    if n_prev:
        in_specs.append(pl.BlockSpec((n_prev, nb, None, HEAD_DIM, HEAD_DIM), st_map))
        args.append(prev)
    return pl.pallas_call(
        functools.partial(_hgrn_scan_kernel, tc=tc, nseq=nseq, nrb=tt // R, nt=nt, has_state=has_state,
                          n_prev=n_prev),
        grid=grid,
        in_specs=in_specs,
        out_specs=[rspec, pl.BlockSpec((n_prev + 1, nb, None, HEAD_DIM, HEAD_DIM), st_map)],
        out_shape=[jax.ShapeDtypeStruct((nh, m, HEAD_DIM), BF16),
                   jax.ShapeDtypeStruct((n_prev + 1, bsz, nh, HEAD_DIM, HEAD_DIM), F32)],
        scratch_shapes=[pltpu.VMEM((HEAD_DIM, HEAD_DIM), F32)],
        compiler_params=_cparams(("parallel", "parallel", "arbitrary")),
        name="hgrn_scan",
    )(*args)


def _hgrn_out_kernel(x_ref, og_ref, w_ref, out_ref, *, nh):
    og = jnp.concatenate([og_ref[hh] for hh in range(nh)], axis=-1)
    out_ref[...] = x_ref[...] + jnp.dot(og, w_ref[...], preferred_element_type=F32)


def _hgrn_out(x, og, w_out, layer_j):
    m, d = x.shape
    nh = og.shape[0]
    tm = min(1024, m)
    return pl.pallas_call(
        functools.partial(_hgrn_out_kernel, nh=nh),
        grid=(m // tm,),
        in_specs=[
            pl.BlockSpec((tm, d), lambda i: (i, 0)),
            pl.BlockSpec((nh, tm, HEAD_DIM), lambda i: (0, i, 0)),
            pl.BlockSpec((None, d, d), lambda i: (layer_j, 0, 0)),
        ],
        out_specs=pl.BlockSpec((tm, d), lambda i: (i, 0)),
        out_shape=jax.ShapeDtypeStruct((m, d), F32),
        compiler_params=_cparams(("parallel",)),
        name="hgrn_out",
    )(x, og, w_out)


def _split3(a):
    hi = a.astype(BF16)
    r1 = a - hi.astype(F32)
    mid = r1.astype(BF16)
    lo = (r1 - mid.astype(F32)).astype(BF16)
    return hi, mid, lo


def _dot_precise(a, b):
    ah, am, al = _split3(a)
    bh, bm, bl = _split3(b)
    d = functools.partial(jnp.dot, preferred_element_type=F32)
    return (d(ah, bh) + (d(ah, bm) + d(am, bh))) + ((d(am, bm) + d(ah, bl)) + d(al, bh))


def _s5_prep_kernel(arc_ref, aic_ref, arr_ref, air_ref, ldt_ref, x1_ref, x2_ref, y1_ref, y2_ref, plt_ref, pls_ref,
                    w1_ref, w2_ref, w3_ref, pw_ref):
    L, GC, P, GL = S5_L, S5_GC, S5_P, S5_GL
    W = L * GC
    TW = L * GL * GC
    lane_t = lax.broadcasted_iota(jnp.int32, (1, W), 1) // GC
    row_s = lax.broadcasted_iota(jnp.int32, (W, 1), 0) // GC
    sgn_l = jnp.where(lax.broadcasted_iota(jnp.int32, (1, 2 * P), 1) < P, -1.0, 1.0)
    sgn_s = jnp.where(lax.broadcasted_iota(jnp.int32, (2 * P, 1), 0) < P, 1.0, -1.0)
    row16 = lax.broadcasted_iota(jnp.int32, (16, 1), 0)
    dot = functools.partial(jnp.dot, preferred_element_type=F32)

    def cmul(ar, ai, br, bi):
        return ar * br - ai * bi, ar * bi + ai * br

    bases = []
    pw = jnp.zeros((16, GL * 2 * P), F32)
    for g in range(GL):
        dt = jnp.exp(ldt_ref[g])
        a_re_c, a_im_c = arc_ref[g], aic_ref[g]
        er = jnp.exp(a_re_c * dt)
        lbr_c, lbi_c = er * jnp.cos(a_im_c * dt), er * jnp.sin(a_im_c * dt)
        a_re_r, a_im_r = arr_ref[g], air_ref[g]
        er_r = jnp.exp(a_re_r * dt)
        lbr_r, lbi_r = er_r * jnp.cos(a_im_r * dt), er_r * jnp.sin(a_im_r * dt)
        den = a_re_r * a_re_r + a_im_r * a_im_r
        nr, ni = lbr_r - 1.0, lbi_r
        f_r = (nr * a_re_r + ni * a_im_r) / den
        f_i = (ni * a_re_r - nr * a_im_r) / den

        pc = [(jnp.ones_like(lbr_c), jnp.zeros_like(lbr_c))]
        pr = [(jnp.ones_like(lbr_r), jnp.zeros_like(lbr_r))]
        for _ in range(L):
            pc.append(cmul(pc[-1][0], pc[-1][1], lbr_c, lbi_c))
            pr.append(cmul(pr[-1][0], pr[-1][1], lbr_r, lbi_r))

        q_re = jnp.zeros((2 * P, W), F32)
        q_im = jnp.zeros((2 * P, W), F32)
        for t in range(L):
            sel = lane_t == t
            q_re = jnp.where(sel, pc[t][0], q_re)
            q_im = jnp.where(sel, pc[t][1], q_im)
        y1, y2 = y1_ref[g], y2_ref[g]
        mp = sgn_s * (q_re * y1) - q_im * y2
        q1_re, q1_im = cmul(q_re, q_im, lbr_c, lbi_c)
        w3g = (sgn_s * (q1_re * y1) - q1_im * y2).astype(BF16)
        z3 = dot(w3g, plt_ref[g]).astype(BF16)
        for ri in range(2):
            w3_ref[ri * GL * P + g * P:ri * GL * P + (g + 1) * P, :] = z3[ri * P:(ri + 1) * P]

        cf_re = jnp.zeros((W, 2 * P), F32)
        cf_im = jnp.zeros((W, 2 * P), F32)
        for s in range(L):
            sel = row_s == s
            cf_re = jnp.where(sel, pr[L - 1 - s][0], cf_re)
            cf_im = jnp.where(sel, pr[L - 1 - s][1], cf_im)
        cf_re, cf_im = cmul(cf_re, cf_im, f_r, f_i)
        x1, x2 = x1_ref[g], x2_ref[g]
        w2g = (cf_re * x1 + (cf_im * sgn_l) * x2).astype(BF16)
        v2 = dot(w2g, pls_ref[g]).astype(BF16)
        for s in range(L):
            w2_ref[s * GL * GC + g * GC:s * GL * GC + (g + 1) * GC, :] = v2[s * GC:(s + 1) * GC]

        bbp = f_r * x1[0:GC] + (f_i * sgn_l) * x2[0:GC]
        base = _dot_precise(bbp, mp).astype(BF16)
        bases.append(dot(base, plt_ref[g]))

        tab = jnp.zeros((16, 2 * P), F32)
        cur = pr[L]
        for k in range(8):
            tab = jnp.where(row16 == k, cur[0], tab)
            tab = jnp.where(row16 == 8 + k, cur[1] * sgn_l, tab)
            cur = cmul(cur[0], cur[1], cur[0], cur[1])
        th, tm_, tl = _split3(tab)
        pls = pls_ref[g]
        pw = pw + ((dot(th, pls) + dot(tm_, pls)) + dot(tl, pls))

    base8 = jnp.concatenate(bases, axis=0)
    lane = lax.broadcasted_iota(jnp.int32, (1, TW), 1)
    blk = GL * GC
    w1_ref[0:blk, :] = base8.astype(BF16)
    for s in range(1, L):
        w1_ref[s * blk:(s + 1) * blk, :] = jnp.where(lane >= s * blk, pltpu.roll(base8, s * blk, 1), 0.0).astype(BF16)
    pw_ref[...] = pw


def _s5_placements():
    L, GC, P, GL = S5_L, S5_GC, S5_P, S5_GL
    plt = np.zeros((GL, L * GC, L * GL * GC), np.float32)
    pls = np.zeros((GL, 2 * P, 2 * GL * P), np.float32)
    for gl in range(GL):
        for t in range(L):
            for c in range(GC):
                plt[gl, t * GC + c, t * GL * GC + gl * GC + c] = 1.0
        for ri in range(2):
            for p in range(P):
                pls[gl, ri * P + p, ri * GL * P + gl * P + p] = 1.0
    return jnp.asarray(plt, BF16), jnp.asarray(pls, BF16)


def _s5_prep(a_re, a_im, log_dt, b_re, b_im, c_re, c_im):
    g, p = a_re.shape
    L, GC, GL = S5_L, S5_GC, S5_GL
    W = L * GC
    TW = L * GL * GC
    SW = 2 * GL * p
    nblk = g // GL
    a_re2 = jnp.concatenate([a_re, a_re], axis=1)
    a_im2 = jnp.concatenate([a_im, a_im], axis=1)
    arc, aic = a_re2[:, :, None], a_im2[:, :, None]
    arr, air = a_re2[:, None, :], a_im2[:, None, :]
    ldt = log_dt[:, None, None]
    btr, bti = jnp.transpose(b_re, (0, 2, 1)), jnp.transpose(b_im, (0, 2, 1))
    x1 = jnp.tile(jnp.concatenate([btr, bti], axis=2), (1, L, 1))
    x2 = jnp.tile(jnp.concatenate([bti, btr], axis=2), (1, L, 1))
    ctr, cti = jnp.transpose(c_re, (0, 2, 1)), jnp.transpose(c_im, (0, 2, 1))
    y1 = jnp.tile(jnp.concatenate([ctr, cti], axis=1), (1, 1, L))
    y2 = jnp.tile(jnp.concatenate([cti, ctr], axis=1), (1, 1, L))
    plt, pls = _s5_placements()

    def spec(shape):
        return pl.BlockSpec((GL,) + shape, lambda i: (i, 0, 0))

    const3 = lambda i: (0, 0, 0)
    out3 = lambda i: (i, 0, 0)
    return pl.pallas_call(
        _s5_prep_kernel,
        grid=(nblk,),
        in_specs=[spec((2 * p, 1)), spec((2 * p, 1)), spec((1, 2 * p)), spec((1, 2 * p)), spec((1, 1)),
                  spec((W, 2 * p)), spec((W, 2 * p)), spec((2 * p, W)), spec((2 * p, W)),
                  pl.BlockSpec(plt.shape, const3), pl.BlockSpec(pls.shape, const3)],
        out_specs=[pl.BlockSpec((None, TW, TW), out3), pl.BlockSpec((None, TW, SW), out3),
                   pl.BlockSpec((None, SW, TW), out3), pl.BlockSpec((None, 16, SW), out3)],
        out_shape=[jax.ShapeDtypeStruct((nblk, TW, TW), BF16), jax.ShapeDtypeStruct((nblk, TW, SW), BF16),
                   jax.ShapeDtypeStruct((nblk, SW, TW), BF16), jax.ShapeDtypeStruct((nblk, 16, SW), F32)],
        compiler_params=_cparams(("parallel",)),
        name="s5_prep",
    )(arc, aic, arr, air, ldt, x1, x2, y1, y2, plt, pls)


def _s5_core_kernel(*refs, nrows, has_state):
    if has_state:
        hn_ref, w1_ref, w2_ref, w3_ref, pw_ref, s0_ref, y_ref, fin_ref = refs
    else:
        hn_ref, w1_ref, w2_ref, w3_ref, pw_ref, y_ref, fin_ref, x_scr, p_scr, s1_scr, sin_scr = refs
    L = S5_L
    half = S5_GL * S5_P
    dot = functools.partial(jnp.dot, preferred_element_type=F32)

    ns = half // 128
    strips = lambda a: [a[:, s * 128:(s + 1) * 128] for s in range(2 * ns)]
    join = lambda parts: jnp.concatenate(parts, axis=1)

    def scale_add(a, k, c):
        out = [None] * (2 * ns)
        for st in range(ns):
            wr = pw_ref[k:k + 1, st * 128:(st + 1) * 128]
            wi = pw_ref[8 + k:9 + k, half + st * 128:half + (st + 1) * 128]
            ar, ai = a[st], a[ns + st]
            out[st] = c[st] + (ar * wr - ai * wi)
            out[ns + st] = c[ns + st] + (ar * wi + ai * wr)
        return out

    u = jnp.concatenate([hn_ref[pl.ds(s, nrows, stride=L), :] for s in range(L)], axis=1).astype(BF16)
    x = dot(u, w2_ref[...])
    y_local = dot(u, w1_ref[...])
    if has_state:
        s_in = s0_ref[...]
        fin_ref[...] = join(scale_add(strips(s_in), 0, strips(x)))
    else:
        n1, n2 = nrows // 2, nrows // 4
        every = range(2 * ns)

        def put(ref, parts, start=None, n=None):
            for s in every:
                if start is None:
                    ref[s] = parts[s]
                else:
                    ref[s, pl.ds(start, n, stride=2), :] = parts[s]

        def alternate(ref, start, n):
            return [ref[s, pl.ds(start, n, stride=2), :] for s in every]

        put(x_scr, strips(x))
        e0, o0 = alternate(x_scr, 0, n1), alternate(x_scr, 1, n1)
        put(p_scr, scale_add(e0, 0, o0))
        e1, o1 = alternate(p_scr, 0, n2), alternate(p_scr, 1, n2)
        z = scale_add(e1, 1, o1)
        row = lax.broadcasted_iota(jnp.int32, (n2, 1), 0)

        def shift(parts, sh):
            if sh % 8 == 0:
                return [jnp.concatenate([jnp.zeros((sh, 128), F32), a[:n2 - sh]], axis=0) for a in parts]
            return [jnp.where(row >= sh, pltpu.roll(a, sh, 0), 0.0) for a in parts]

        sh, k = 1, 2
        while sh < n2:
            z = scale_add(shift(z, sh), k, z)
            sh *= 2
            k += 1
        fin_ref[...] = join([a[n2 - 1:n2, :] for a in z])
        s2 = shift(z, 1)
        put(s1_scr, s2, 0, n2)
        put(s1_scr, scale_add(s2, 1, e1), 1, n2)
        s1 = [s1_scr[s] for s in every]
        put(sin_scr, s1, 0, n1)
        put(sin_scr, scale_add(s1, 0, e0), 1, n1)
        s_in = join([sin_scr[s] for s in every])
    y = y_local + dot(s_in.astype(BF16), w3_ref[...])
    lanes = y_ref.shape[1]
    for t in range(L):
        y_ref[pl.ds(t, nrows, stride=L), :] = y[:, t * lanes:(t + 1) * lanes]


def _s5_core(hn, w1, w2, w3, pw, s0, *, bsz, seq):
    m, d = hn.shape
    L = S5_L
    nblk, tw, sw = w2.shape
    lanes = d // nblk
    has_state = s0 is not None
    assert has_state == (seq == L) and seq % L == 0 and seq // L <= 256
    if has_state:
        tile, nb = m, 1
        fin_shape = jax.ShapeDtypeStruct((bsz, nblk * sw), F32)
        fin_spec = pl.BlockSpec((bsz, sw), lambda g, b: (0, g))
    else:
        tile, nb = seq, bsz
        fin_shape = jax.ShapeDtypeStruct((bsz, 1, nblk * sw), F32)
        fin_spec = pl.BlockSpec((None, 1, sw), lambda g, b: (b, 0, g))
    nrows = tile // L
    wmap = lambda g, b: (g, 0, 0)
    in_specs = [
        pl.BlockSpec((tile, lanes), lambda g, b: (b, g)),
        pl.BlockSpec((None, tw, tw), wmap), pl.BlockSpec((None, tw, sw), wmap),
        pl.BlockSpec((None, sw, tw), wmap), pl.BlockSpec((None, 16, sw), wmap),
    ]
    args = [hn, w1, w2, w3, pw]
    if has_state:
        in_specs.append(pl.BlockSpec((bsz, sw), lambda g, b: (0, g)))
        args.append(s0)
    return pl.pallas_call(
        functools.partial(_s5_core_kernel, nrows=nrows, has_state=has_state),
        grid=(nblk, nb),
        in_specs=in_specs,
        out_specs=[pl.BlockSpec((tile, lanes), lambda g, b: (b, g)), fin_spec],
        out_shape=[jax.ShapeDtypeStruct((m, d), F32), fin_shape],
        scratch_shapes=[] if has_state else [pltpu.VMEM((sw // 128, nrows // r, 128), F32) for r in (1, 2, 2, 1)],
        compiler_params=_cparams(("parallel", "parallel")),
        name="s5_core",
    )(*args)


def _gelu_tanh(x):
    return 0.5 * x * (1.0 + jnp.tanh(0.7978845608028654 * (x + 0.044715 * (x * x * x))))


def _s5_out_kernel(x_ref, ln_ref, y_ref, d_ref, w_ref, out_ref, *, d):
    x = x_ref[...]
    h = _rms(x, ln_ref[...])
    y = y_ref[...].astype(F32) + d_ref[...] * h
    z = _gelu_tanh(y).astype(BF16)
    ag = jnp.dot(z, w_ref[...], preferred_element_type=F32)
    out_ref[...] = x + ag[:, :d] * _sigmoid(ag[:, d:])


def _s5_out(x, ln, y, d_skip, w_glu, layer_j):
    m, d = x.shape
    tm = min(1024, m)
    return pl.pallas_call(
        functools.partial(_s5_out_kernel, d=d),
        grid=(m // tm,),
        in_specs=[
            pl.BlockSpec((tm, d), lambda i: (i, 0)),
            pl.BlockSpec((1, d), lambda i: (0, 0)),
            pl.BlockSpec((tm, d), lambda i: (i, 0)),
            pl.BlockSpec((1, d), lambda i: (0, 0)),
            pl.BlockSpec((None, d, 2 * d), lambda i: (layer_j, 0, 0)),
        ],
        out_specs=pl.BlockSpec((tm, d), lambda i: (i, 0)),
        out_shape=jax.ShapeDtypeStruct((m, d), F32),
        compiler_params=_cparams(("parallel",)),
        name="s5_out",
    )(x, ln, y, d_skip, w_glu)


def _unpack_state(fin, bsz):
    f = fin.reshape(bsz, -1, 2, S5_GL, S5_P)
    return f[:, :, 0].reshape(bsz, -1, S5_P), f[:, :, 1].reshape(bsz, -1, S5_P)


def _layer(i, st, wts, w_bf):
    j = i // 2
    depth = wts["ln_ffn1"].shape[0]
    x, bsz, seq = st["x"], st["bsz"], st["seq"]
    is_hgrn = i % 2 == 0

    def ffn(name, x, ln, gain2, mix=None, **kw):
        key = (name, i)
        kw = dict(dict(hn_dtype=None, final_norm=False), **kw)
        if key in w_bf:
            w_in, w_out = w_bf[key]
            out, hn, _ = _ffn_resident(x, ln, w_in, w_out, gain2, mix=mix, **kw)
        else:
            if mix is not None:
                x = _hgrn_out(x, *mix)
            out, hn, w_bf[key] = _ffn_cast(x, ln, wts[name + "_w_in"], wts[name + "_w_out"], i, gain2, **kw)
        return out, hn

    mix = None

    x, hn = ffn("ffn1", x, wts["ln_ffn1"][i:i + 1], wts["ln_mix"][i:i + 1], hn_dtype=BF16 if is_hgrn else F32)
    if is_hgrn:
        cum, masks = _level_consts(min(seq, ROW_BLOCK))
        q, k, v, b, sg = _hgrn_gates(hn, wts["hgrn_w_in"], j, wts["hgrn_lb_logits"], i, cum)
        og, st["new_hgrn"] = _hgrn_scan(q, k, v, b, sg, wts["hgrn_gnorm"][j:j + 1], masks, st["hgrn"], j,
                                        st["new_hgrn"], bsz=bsz, seq=seq)
        mix = (og, wts["hgrn_w_out"], j)
    else:
        w1, w2, w3, pw = wts["s5_mats"][j]
        s0 = None
        if st["re"] is not None:
            pack = lambda a: a[j].reshape(bsz, -1, 1, S5_GL, S5_P)
            s0 = jnp.concatenate([pack(st["re"]), pack(st["im"])], axis=2).reshape(bsz, -1)
        y, fin = _s5_core(hn, w1, w2, w3, pw, s0, bsz=bsz, seq=seq)
        f_re, f_im = _unpack_state(fin, bsz)
        st["new_re"].append(f_re)
        st["new_im"].append(f_im)
        x = _s5_out(x, wts["ln_mix"][i:i + 1], y, wts["s5_d"][j:j + 1], wts["s5_w_glu"], j)
    last = i == depth - 1
    gain2 = wts["ln_final"] if last else wts["ln_ffn2"][i:i + 1]
    x, _ = ffn("ffn2", x, wts["ln_ffn2"][i:i + 1], gain2, mix=mix, final_norm=last)
    st["x"] = x


def kernel(x_prompt, x_sample, state_hgrn, state_s5_re, state_s5_im, ln_ffn1, ffn1_w_in, ffn1_w_out, ln_mix, ln_ffn2, ffn2_w_in, ffn2_w_out, hgrn_lb_logits, hgrn_w_in, hgrn_gnorm, hgrn_w_out, s5_a_re, s5_a_im, s5_log_dt, s5_b_re, s5_b_im, s5_c_re, s5_c_im, s5_d, s5_w_glu, ln_final):
    depth, d = ln_ffn1.shape
    wts = dict(
        ln_ffn1=ln_ffn1, ln_mix=ln_mix, ln_ffn2=ln_ffn2, ln_final=ln_final.reshape(1, -1),
        ffn1_w_in=ffn1_w_in, ffn1_w_out=ffn1_w_out, ffn2_w_in=ffn2_w_in, ffn2_w_out=ffn2_w_out,
        hgrn_lb_logits=hgrn_lb_logits, hgrn_w_in=hgrn_w_in.astype(BF16), hgrn_gnorm=hgrn_gnorm,
        hgrn_w_out=hgrn_w_out.astype(BF16), s5_d=s5_d, s5_w_glu=s5_w_glu.astype(BF16),
        s5_mats=[_s5_prep(s5_a_re[j], s5_a_im[j], s5_log_dt[j], s5_b_re[j], s5_b_im[j], s5_c_re[j], s5_c_im[j])
                 for j in range(s5_a_re.shape[0])],
    )

    def stream(x3, hgrn, re, im):
        bsz, seq, _ = x3.shape
        return dict(x=x3.reshape(bsz * seq, d), bsz=bsz, seq=seq, hgrn=hgrn, re=re, im=im,
                    new_hgrn=None, new_re=[], new_im=[])

    sample = stream(x_sample, state_hgrn, state_s5_re, state_s5_im)
    prompt = stream(x_prompt, None, None, None)
    w_bf = {}
    for i in range(depth):
        _layer(i, sample, wts, w_bf)
        _layer(i, prompt, wts, w_bf)

    def outs(st, x3):
        return (st["x"].reshape(x3.shape), st["new_hgrn"], jnp.stack(st["new_re"]), jnp.stack(st["new_im"]))

    y_p, hg_p, re_p, im_p = outs(prompt, x_prompt)
    y_s, hg_s, re_s, im_s = outs(sample, x_sample)
    return (y_p, y_s, hg_p, re_p, im_p, hg_s, re_s, im_s)
```

```python
import functools

import numpy as np
import jax
import jax.numpy as jnp
from jax import lax
from jax.experimental import pallas as pl
from jax.experimental.pallas import tpu as pltpu

F32 = jnp.float32
BF16 = jnp.bfloat16

EPS = 1e-6
GATE_FLOOR = 1e-30
MACARON_SCALE = 0.5
LB_CLIP = 1.0 - 1e-4

HEAD_DIM = 128
S5_GC = 16
S5_P = 64
S5_L = 8
S5_GL = 8
S5_SEQ_PER_STEP = 4
ROW_BLOCK = 128
FFN_TM = 1024
FFN_TF = 256
HGRN_TT = 2048
HGRN_GATES_TM = 1024
HGRN_TT_SHORT = 512
VMEM_LIMIT = 52 * 1024 * 1024


def _cparams(sem):
    return pltpu.CompilerParams(dimension_semantics=sem, vmem_limit_bytes=VMEM_LIMIT)


def _rms(x, gain):
    return x * lax.rsqrt(jnp.mean(x * x, axis=-1, keepdims=True) + EPS) * gain


def _sigmoid(x):
    return 1.0 / (1.0 + jnp.exp(-x))


def _ffn_cast_kernel(x_ref, ln_ref, wa_ref, wb_ref, wout_ref, g2_ref, *rest, nf, hn_dtype, final_norm):
    rest = list(rest)
    out_ref = rest.pop(0)
    hn_ref = rest.pop(0) if hn_dtype is not None else None
    wa_bf_ref, wb_bf_ref, wout_bf_ref, h_scr, acc_scr = rest
    j = pl.program_id(1)

    @pl.when(j == 0)
    def _():
        h_scr[...] = _rms(x_ref[...], ln_ref[...]).astype(BF16)
        acc_scr[...] = jnp.zeros_like(acc_scr)

    wa, wb, wo = wa_ref[...].astype(BF16), wb_ref[...].astype(BF16), wout_ref[...].astype(BF16)
    wa_bf_ref[...] = wa
    wb_bf_ref[...] = wb
    wout_bf_ref[...] = wo
    h = h_scr[...]
    a = jnp.dot(h, wa, preferred_element_type=F32)
    b = jnp.dot(h, wb, preferred_element_type=F32)
    g = (a * _sigmoid(a) * b).astype(BF16)
    acc_scr[...] += jnp.dot(g, wo, preferred_element_type=F32)

    @pl.when(j == nf - 1)
    def _():
        y = x_ref[...] + MACARON_SCALE * acc_scr[...]
        if final_norm:
            out_ref[...] = _rms(y, g2_ref[...])
        else:
            out_ref[...] = y
        if hn_ref is not None:
            hn_ref[...] = _rms(y, g2_ref[...]).astype(hn_ref.dtype)


def _ffn_resident_kernel(x_ref, ln_ref, wa_ref, wb_ref, wout_ref, g2_ref, *rest, tf, hn_dtype, final_norm, mix_heads):
    rest = list(rest)
    if mix_heads:
        og_ref, wmix_ref = rest[:2]
        del rest[:2]
    out_ref = rest.pop(0)
    hn_ref = rest.pop(0) if hn_dtype is not None else None
    g_scr, = rest
    x = x_ref[...]
    if mix_heads:
        og = jnp.concatenate([og_ref[hh] for hh in range(mix_heads)], axis=-1)
        x = x + jnp.dot(og, wmix_ref[...], preferred_element_type=F32)
    h = _rms(x, ln_ref[...]).astype(BF16)
    for j in range(g_scr.shape[1] // tf):
        cols = slice(j * tf, (j + 1) * tf)
        a = jnp.dot(h, wa_ref[:, cols], preferred_element_type=F32)
        b = jnp.dot(h, wb_ref[:, cols], preferred_element_type=F32)
        g_scr[:, cols] = (a * _sigmoid(a) * b).astype(BF16)
    y = x + MACARON_SCALE * jnp.dot(g_scr[...], wout_ref[...], preferred_element_type=F32)
    if final_norm:
        out_ref[...] = _rms(y, g2_ref[...])
    else:
        out_ref[...] = y
    if hn_ref is not None:
        hn_ref[...] = _rms(y, g2_ref[...]).astype(hn_ref.dtype)


def _ffn_resident(x, ln, w_in, w_out, gain2, *, hn_dtype, final_norm, mix=None):
    m, d = x.shape
    wa, wb = w_in
    f = w_out.shape[1]
    tm = min(FFN_TM, m)
    row = lambda i: (i, 0)
    once = pl.Buffered(1)
    out_shape = [jax.ShapeDtypeStruct((m, d), F32)]
    out_specs = [pl.BlockSpec((tm, d), row)]
    if hn_dtype is not None:
        out_shape.append(jax.ShapeDtypeStruct((m, d), hn_dtype))
        out_specs.append(pl.BlockSpec((tm, d), row))
    in_specs = [
        pl.BlockSpec((tm, d), row),
        pl.BlockSpec((1, d), lambda i: (0, 0)),
        pl.BlockSpec((None, d, f), lambda i: (0, 0, 0), pipeline_mode=once),
        pl.BlockSpec((None, d, f), lambda i: (0, 0, 0), pipeline_mode=once),
        pl.BlockSpec((None, f, d), lambda i: (0, 0, 0), pipeline_mode=once),
        pl.BlockSpec((1, d), lambda i: (0, 0)),
    ]
    args = [x, ln, wa, wb, w_out, gain2]
    mix_heads = 0
    if mix is not None:
        og, w_mix, mix_layer = mix
        mix_heads = og.shape[0]
        in_specs += [pl.BlockSpec((mix_heads, tm, og.shape[2]), lambda i: (0, i, 0)),
                     pl.BlockSpec((None, d, d), lambda i: (mix_layer, 0, 0), pipeline_mode=once)]
        args += [og, w_mix]
    res = pl.pallas_call(
        functools.partial(_ffn_resident_kernel, tf=FFN_TF, hn_dtype=hn_dtype, final_norm=final_norm,
                          mix_heads=mix_heads),
        grid=(m // tm,),
        in_specs=in_specs,
        out_specs=out_specs,
        out_shape=out_shape,
        scratch_shapes=[pltpu.VMEM((tm, f), BF16)],
        compiler_params=_cparams(("parallel",)),
        name="ffn",
    )(*args)
    return res[0], (res[1] if hn_dtype is not None else None), None


def _ffn_cast(x, ln, w_in, w_out, layer, gain2, *, hn_dtype, final_norm):
    m, d = x.shape
    f = w_out.shape[1]
    tf = FFN_TF
    nf = f // tf
    assert m <= FFN_TM
    row = lambda i, j: (0, 0)
    out_shape = [jax.ShapeDtypeStruct((m, d), F32)]
    out_specs = [pl.BlockSpec((m, d), row)]
    if hn_dtype is not None:
        out_shape.append(jax.ShapeDtypeStruct((m, d), hn_dtype))
        out_specs.append(pl.BlockSpec((m, d), row))
    out_shape += [jax.ShapeDtypeStruct((1, d, f), BF16)] * 2 + [jax.ShapeDtypeStruct((1, f, d), BF16)]
    out_specs += [pl.BlockSpec((None, d, tf), lambda i, j: (0, 0, j))] * 2
    out_specs += [pl.BlockSpec((None, tf, d), lambda i, j: (0, j, 0))]
    res = pl.pallas_call(
        functools.partial(_ffn_cast_kernel, nf=nf, hn_dtype=hn_dtype, final_norm=final_norm),
        grid=(1, nf),
        in_specs=[
            pl.BlockSpec((m, d), row),
            pl.BlockSpec((1, d), row),
            pl.BlockSpec((None, d, tf), lambda i, j: (layer, 0, j)),
            pl.BlockSpec((None, d, tf), lambda i, j: (layer, 0, nf + j)),
            pl.BlockSpec((None, tf, d), lambda i, j: (layer, j, 0)),
            pl.BlockSpec((1, d), row),
        ],
        out_specs=out_specs,
        out_shape=out_shape,
        scratch_shapes=[pltpu.VMEM((m, d), BF16), pltpu.VMEM((m, d), F32)],
        compiler_params=_cparams(("arbitrary", "arbitrary")),
        name="ffn_cast",
    )(x, ln, w_in, w_in, w_out, gain2)
    res = list(res)
    out = res.pop(0)
    hn = res.pop(0) if hn_dtype is not None else None
    return out, hn, ((res[0], res[1]), res[2])


def _hgrn_gates_kernel(hn_ref, w_ref, lbl_ref, cum_ref, q_ref, k_ref, v_ref, b_ref, sg_ref, *, layer, d, nh):
    h = hn_ref[...]
    logits = lbl_ref[...]
    e = jnp.exp(logits - jnp.max(logits, axis=0, keepdims=True))
    p = e / jnp.sum(e, axis=0, keepdims=True)
    cum = p[0:1]
    for r in range(1, layer + 1):
        cum = cum + p[r:r + 1]
    lb = jnp.clip(cum - p[0:1], 0.0, LB_CLIP)

    def heads(ref, val):
        for hh in range(nh):
            ref[hh] = val[:, hh * HEAD_DIM:(hh + 1) * HEAD_DIM].astype(ref.dtype)

    pq = jnp.dot(h, w_ref[:, 0:d], preferred_element_type=F32)
    heads(q_ref, pq)
    pf = jnp.dot(h, w_ref[:, d:2 * d], preferred_element_type=F32)
    ez = jnp.exp(-jnp.abs(pf))
    r = 1.0 / (1.0 + ez)
    pos = pf >= 0.0
    sig_p = jnp.where(pos, r, ez * r)
    sig_n = jnp.where(pos, ez * r, r)
    fg = lb + (1.0 - lb) * sig_p
    hi, mid, lo = _split3(jnp.log(jnp.maximum(fg, GATE_FLOOR)))
    cum = cum_ref[...]
    dot = functools.partial(jnp.dot, preferred_element_type=F32)
    R = ROW_BLOCK
    for rb in range(h.shape[0] // R):
        rows = slice(rb * R, (rb + 1) * R)
        b = (dot(cum, hi[rows]) + dot(cum, mid[rows])) + dot(cum, lo[rows])
        for hh in range(nh):
            b_ref[hh, rows, :] = b[:, hh * HEAD_DIM:(hh + 1) * HEAD_DIM]
    heads(k_ref, (1.0 - lb) * sig_n)
    pv = jnp.dot(h, w_ref[:, 2 * d:3 * d], preferred_element_type=F32)
    heads(v_ref, pv)
    pg = jnp.dot(h, w_ref[:, 3 * d:4 * d], preferred_element_type=F32)
    heads(sg_ref, pg * _sigmoid(pg))


def _hgrn_gates(hn, w_in, layer_j, lb_logits, layer, cum):
    m, d = hn.shape
    nh = d // HEAD_DIM
    tm = min(HGRN_GATES_TM, m)
    hm = lambda dt: jax.ShapeDtypeStruct((nh, m, HEAD_DIM), dt)
    hspec = pl.BlockSpec((nh, tm, HEAD_DIM), lambda i: (0, i, 0))
    return pl.pallas_call(
        functools.partial(_hgrn_gates_kernel, layer=layer, d=d, nh=nh),
        grid=(m // tm,),
        in_specs=[
            pl.BlockSpec((tm, d), lambda i: (i, 0)),
            pl.BlockSpec((None, d, 4 * d), lambda i: (layer_j, 0, 0)),
            pl.BlockSpec(lb_logits.shape, lambda i: (0, 0)),
            pl.BlockSpec(cum.shape, lambda i: (0, 0)),
        ],
        out_specs=[hspec] * 5,
        out_shape=[hm(BF16), hm(BF16), hm(BF16), hm(F32), hm(BF16)],
        compiler_params=_cparams(("parallel",)),
        name="hgrn_gates",
    )(hn, w_in, lb_logits, cum)


def _anchor(b, m):
    R, n = b.shape
    if m >= 8:
        parts = [jnp.broadcast_to(b[blk * 2 * m + m - 1:blk * 2 * m + m, :], (2 * m, n)) for blk in range(R // (2 * m))]
        return parts[0] if len(parts) == 1 else jnp.concatenate(parts, axis=0)
    if m == 1:
        odd = jnp.bitwise_and(lax.broadcasted_iota(jnp.int32, (R, 1), 0), 1) == 1
        return jnp.where(odd, pltpu.roll(b, 1, 0), b)
    b3 = b.reshape(R // 8, 8, n)
    if m == 4:
        a3 = jnp.broadcast_to(b3[:, 3:4, :], b3.shape)
    else:
        sub = lax.broadcasted_iota(jnp.int32, (1, 8, 1), 1)
        a3 = jnp.where(sub < 4, jnp.broadcast_to(b3[:, 1:2, :], b3.shape), jnp.broadcast_to(b3[:, 5:6, :], b3.shape))
    return a3.reshape(R, n)


def _block_end(b, tc):
    R, n = b.shape
    if tc == 8:
        b3 = b.reshape(R // 8, 8, n)
        return jnp.broadcast_to(b3[:, 7:8, :], b3.shape).reshape(R, n)
    parts = [jnp.broadcast_to(b[blk * tc + tc - 1:blk * tc + tc, :], (tc, n)) for blk in range(R // tc)]
    return parts[0] if len(parts) == 1 else jnp.concatenate(parts, axis=0)


def _hgrn_scan_kernel(*refs, tc, nseq, nrb, nt, has_state, n_prev):
    refs = list(refs)
    q_ref, k_ref, v_ref, b_ref, sg_ref, gn_ref, msk_ref = refs[:7]
    del refs[:7]
    s0_ref = refs.pop(0) if has_state else None
    prev_ref = refs.pop(0) if n_prev else None
    o_ref, st_ref, s_scr = refs
    R = ROW_BLOCK
    t_idx = pl.program_id(2)
    dot = functools.partial(jnp.dot, preferred_element_type=F32)
    dn = (((1,), (1,)), ((), ()))
    dn0 = (((0,), (0,)), ((), ()))

    if nseq == 1:
        @pl.when(t_idx == 0)
        def _():
            if has_state:
                s_scr[...] = s0_ref[...].reshape(s_scr.shape)
            else:
                s_scr[...] = jnp.zeros_like(s_scr)

    seq_of_row = lax.broadcasted_iota(jnp.int32, (R, 1), 0) // tc
    gn = gn_ref[...]

    def local_part(rb):
        sl = pl.ds(rb * R, R)
        q = q_ref[sl, :].astype(F32)
        kk = k_ref[sl, :].astype(F32)
        vb = v_ref[sl, :]
        b = b_ref[sl, :]
        scores = lax.dot_general(q_ref[sl, :], k_ref[sl, :], dn, preferred_element_type=F32) * msk_ref[0]
        m, level = 1, 1
        while m < min(tc, 8):
            w = jnp.exp(-jnp.abs(b - _anchor(b, m)))
            s_m = lax.dot_general((q * w).astype(BF16), (kk * w).astype(BF16), dn, preferred_element_type=F32)
            scores = scores + s_m * msk_ref[level]
            m *= 2
            level += 1
        while m < tc:
            nblk = R // (2 * m)
            q_parts, k_parts, spans = [], [], []
            for blk in range(nblk):
                lo, mid, hi = blk * 2 * m, blk * 2 * m + m, (blk + 1) * 2 * m
                anchor = b[mid - 1:mid, :]
                q_parts.append(q[mid:hi] * jnp.exp(b[mid:hi] - anchor))
                k_parts += [kk[lo:mid] * jnp.exp(anchor - b[lo:mid]), jnp.zeros((m, HEAD_DIM), F32)]
                spans.append((lo, mid, hi))
            s_m = lax.dot_general(jnp.concatenate(q_parts, axis=0).astype(BF16),
                                  jnp.concatenate(k_parts, axis=0).astype(BF16), dn, preferred_element_type=F32)
            rows = []
            for blk, (lo, mid, hi) in enumerate(spans):
                piece = s_m[blk * m:(blk + 1) * m]
                if nblk > 1:
                    piece = piece * msk_ref[level, mid:hi, :]
                rows += [scores[lo:mid], scores[mid:hi] + piece]
            scores = jnp.concatenate(rows, axis=0)
            m *= 2
            level += 1
        e_end = _block_end(b, tc)
        qe = q * jnp.exp(b)
        ke = kk * jnp.exp(e_end - b)
        if nseq > 1:
            qe = jnp.concatenate([jnp.where(seq_of_row == i, qe, 0.0) for i in range(nseq)], axis=1)
            ke = jnp.concatenate([jnp.where(seq_of_row == i, ke, 0.0) for i in range(nseq)], axis=1)
        ds = lax.dot_general(ke.astype(BF16), vb, dn0, preferred_element_type=F32)
        decs = []
        for i in range(nseq):
            dec_row = jnp.exp(e_end[i * tc:i * tc + 1, :])
            decs.append(jnp.transpose(jnp.broadcast_to(dec_row, (HEAD_DIM, HEAD_DIM))))
        dec = decs[0] if nseq == 1 else jnp.concatenate(decs, axis=0)
        return jnp.concatenate([scores.astype(BF16), qe.astype(BF16)], axis=1), vb, ds, dec

    parts = [local_part(rb) for rb in range(nrb)]

    s_run = s_scr[...] if nseq == 1 else None
    for rb in range(nrb):
        sq, vb, ds, dec = parts[rb]
        sl = pl.ds(rb * R, R)
        if nseq == 1:
            s_in = s_run
        else:
            s_in = s0_ref[rb * nseq:(rb + 1) * nseq].reshape(nseq * HEAD_DIM, HEAD_DIM)
        o = dot(sq, jnp.concatenate([vb, s_in.astype(BF16)], axis=0))
        s_new = dec * s_in + ds
        on = o * lax.rsqrt(jnp.mean(o * o, axis=-1, keepdims=True) + EPS) * gn
        o_ref[sl, :] = (on * sg_ref[sl, :].astype(F32)).astype(o_ref.dtype)
        if nseq == 1:
            s_run = s_new
        else:
            st_ref[n_prev, rb * nseq:(rb + 1) * nseq] = s_new.reshape(nseq, HEAD_DIM, HEAD_DIM)

    if nseq == 1:
        s_scr[...] = s_run

        @pl.when(t_idx == nt - 1)
        def _():
            st_ref[n_prev] = s_run.reshape(st_ref.shape[1:])
            if n_prev:
                st_ref[0:n_prev] = prev_ref[...]
    elif n_prev:
        st_ref[0:n_prev] = prev_ref[...]


def _level_consts(tc):
    R = ROW_BLOCK
    t = np.arange(R)[:, None]
    s = np.arange(R)[None, :]
    cum = ((s <= t) & (t // tc == s // tc)).astype(np.float32)
    x = t ^ s
    masks = [t == s]
    m = 1
    while m < tc:
        masks.append((t > s) & (x >= m) & (x < 2 * m))
        m *= 2
    return jnp.asarray(cum, BF16), jnp.asarray(np.stack(masks), F32)


def _hgrn_scan(q, k, v, b, sg, gnorm, masks, state, layer_j, prev, *, bsz, seq):
    nh, m, _ = q.shape
    R = ROW_BLOCK
    has_state = state is not None
    n_prev = 0 if prev is None else prev.shape[0]
    if seq >= R:
        tc, nseq = R, 1
        tt = min(HGRN_TT, seq)
        nt = seq // tt
        nb = 1
        grid = (nh, bsz, nt)
        row_map = lambda h, b, t: (h, b * nt + t, 0)
    else:
        assert has_state
        tc, nseq = seq, R // seq
        tt = min(HGRN_TT_SHORT, m)
        nt = 1
        nb = tt // seq
        grid = (nh, m // tt, 1)
        row_map = lambda h, b, t: (h, b, 0)
    rspec = pl.BlockSpec((None, tt, HEAD_DIM), row_map)
    const2 = lambda h, b, t: (0, 0)
    in_specs = [rspec] * 5 + [pl.BlockSpec((1, HEAD_DIM), const2), pl.BlockSpec(masks.shape, lambda h, b, t: (0, 0, 0))]
    args = [q, k, v, b, sg, gnorm, masks]
    if has_state:
        in_specs.append(pl.BlockSpec((None, nb, None, HEAD_DIM, HEAD_DIM), lambda h, b, t: (layer_j, b, h, 0, 0)))
        args.append(state)
    st_map = lambda h, b, t: (0, b, h, 0, 0)
    if n_prev:
        in_specs.append(pl.BlockSpec((n_prev, nb, None, HEAD_DIM, HEAD_DIM), st_map))
        args.append(prev)
    return pl.pallas_call(
        functools.partial(_hgrn_scan_kernel, tc=tc, nseq=nseq, nrb=tt // R, nt=nt, has_state=has_state,
                          n_prev=n_prev),
        grid=grid,
        in_specs=in_specs,
        out_specs=[rspec, pl.BlockSpec((n_prev + 1, nb, None, HEAD_DIM, HEAD_DIM), st_map)],
        out_shape=[jax.ShapeDtypeStruct((nh, m, HEAD_DIM), BF16),
                   jax.ShapeDtypeStruct((n_prev + 1, bsz, nh, HEAD_DIM, HEAD_DIM), F32)],
        scratch_shapes=[pltpu.VMEM((HEAD_DIM, HEAD_DIM), F32)],
        compiler_params=_cparams(("parallel", "parallel", "arbitrary")),
        name="hgrn_scan",
    )(*args)


def _hgrn_out_kernel(x_ref, og_ref, w_ref, out_ref, *, nh):
    og = jnp.concatenate([og_ref[hh] for hh in range(nh)], axis=-1)
    out_ref[...] = x_ref[...] + jnp.dot(og, w_ref[...], preferred_element_type=F32)


def _hgrn_out(x, og, w_out, layer_j):
    m, d = x.shape
    nh = og.shape[0]
    tm = min(1024, m)
    return pl.pallas_call(
        functools.partial(_hgrn_out_kernel, nh=nh),
        grid=(m // tm,),
        in_specs=[
            pl.BlockSpec((tm, d), lambda i: (i, 0)),
            pl.BlockSpec((nh, tm, HEAD_DIM), lambda i: (0, i, 0)),
            pl.BlockSpec((None, d, d), lambda i: (layer_j, 0, 0)),
        ],
        out_specs=pl.BlockSpec((tm, d), lambda i: (i, 0)),
        out_shape=jax.ShapeDtypeStruct((m, d), F32),
        compiler_params=_cparams(("parallel",)),
        name="hgrn_out",
    )(x, og, w_out)


def _split3(a):
    hi = a.astype(BF16)
    r1 = a - hi.astype(F32)
    mid = r1.astype(BF16)
    lo = (r1 - mid.astype(F32)).astype(BF16)
    return hi, mid, lo


def _dot_precise(a, b):
    ah, am, al = _split3(a)
    bh, bm, bl = _split3(b)
    d = functools.partial(jnp.dot, preferred_element_type=F32)
    return (d(ah, bh) + (d(ah, bm) + d(am, bh))) + ((d(am, bm) + d(ah, bl)) + d(al, bh))


def _s5_prep_kernel(arc_ref, aic_ref, arr_ref, air_ref, ldt_ref, x1_ref, x2_ref, y1_ref, y2_ref, plt_ref, pls_ref,
                    w1_ref, w2_ref, w3_ref, pw_ref):
    L, GC, P, GL = S5_L, S5_GC, S5_P, S5_GL
    W = L * GC
    TW = L * GL * GC
    lane_t = lax.broadcasted_iota(jnp.int32, (1, W), 1) // GC
    row_s = lax.broadcasted_iota(jnp.int32, (W, 1), 0) // GC
    sgn_l = jnp.where(lax.broadcasted_iota(jnp.int32, (1, 2 * P), 1) < P, -1.0, 1.0)
    sgn_s = jnp.where(lax.broadcasted_iota(jnp.int32, (2 * P, 1), 0) < P, 1.0, -1.0)
    row16 = lax.broadcasted_iota(jnp.int32, (16, 1), 0)
    dot = functools.partial(jnp.dot, preferred_element_type=F32)

    def cmul(ar, ai, br, bi):
        return ar * br - ai * bi, ar * bi + ai * br

    bases = []
    pw = jnp.zeros((16, GL * 2 * P), F32)
    for g in range(GL):
        dt = jnp.exp(ldt_ref[g])
        a_re_c, a_im_c = arc_ref[g], aic_ref[g]
        er = jnp.exp(a_re_c * dt)
        lbr_c, lbi_c = er * jnp.cos(a_im_c * dt), er * jnp.sin(a_im_c * dt)
        a_re_r, a_im_r = arr_ref[g], air_ref[g]
        er_r = jnp.exp(a_re_r * dt)
        lbr_r, lbi_r = er_r * jnp.cos(a_im_r * dt), er_r * jnp.sin(a_im_r * dt)
        den = a_re_r * a_re_r + a_im_r * a_im_r
        nr, ni = lbr_r - 1.0, lbi_r
        f_r = (nr * a_re_r + ni * a_im_r) / den
        f_i = (ni * a_re_r - nr * a_im_r) / den

        pc = [(jnp.ones_like(lbr_c), jnp.zeros_like(lbr_c))]
        pr = [(jnp.ones_like(lbr_r), jnp.zeros_like(lbr_r))]
        for _ in range(L):
            pc.append(cmul(pc[-1][0], pc[-1][1], lbr_c, lbi_c))
            pr.append(cmul(pr[-1][0], pr[-1][1], lbr_r, lbi_r))

        q_re = jnp.zeros((2 * P, W), F32)
        q_im = jnp.zeros((2 * P, W), F32)
        for t in range(L):
            sel = lane_t == t
            q_re = jnp.where(sel, pc[t][0], q_re)
            q_im = jnp.where(sel, pc[t][1], q_im)
        y1, y2 = y1_ref[g], y2_ref[g]
        mp = sgn_s * (q_re * y1) - q_im * y2
        q1_re, q1_im = cmul(q_re, q_im, lbr_c, lbi_c)
        w3g = (sgn_s * (q1_re * y1) - q1_im * y2).astype(BF16)
        z3 = dot(w3g, plt_ref[g]).astype(BF16)
        for ri in range(2):
            w3_ref[ri * GL * P + g * P:ri * GL * P + (g + 1) * P, :] = z3[ri * P:(ri + 1) * P]

        cf_re = jnp.zeros((W, 2 * P), F32)
        cf_im = jnp.zeros((W, 2 * P), F32)
        for s in range(L):
            sel = row_s == s
            cf_re = jnp.where(sel, pr[L - 1 - s][0], cf_re)
            cf_im = jnp.where(sel, pr[L - 1 - s][1], cf_im)
        cf_re, cf_im = cmul(cf_re, cf_im, f_r, f_i)
        x1, x2 = x1_ref[g], x2_ref[g]
        w2g = (cf_re * x1 + (cf_im * sgn_l) * x2).astype(BF16)
        v2 = dot(w2g, pls_ref[g]).astype(BF16)
        for s in range(L):
            w2_ref[s * GL * GC + g * GC:s * GL * GC + (g + 1) * GC, :] = v2[s * GC:(s + 1) * GC]

        bbp = f_r * x1[0:GC] + (f_i * sgn_l) * x2[0:GC]
        base = _dot_precise(bbp, mp).astype(BF16)
        bases.append(dot(base, plt_ref[g]))

        tab = jnp.zeros((16, 2 * P), F32)
        cur = pr[L]
        for k in range(8):
            tab = jnp.where(row16 == k, cur[0], tab)
            tab = jnp.where(row16 == 8 + k, cur[1] * sgn_l, tab)
            cur = cmul(cur[0], cur[1], cur[0], cur[1])
        th, tm_, tl = _split3(tab)
        pls = pls_ref[g]
        pw = pw + ((dot(th, pls) + dot(tm_, pls)) + dot(tl, pls))

    base8 = jnp.concatenate(bases, axis=0)
    lane = lax.broadcasted_iota(jnp.int32, (1, TW), 1)
    blk = GL * GC
    w1_ref[0:blk, :] = base8.astype(BF16)
    for s in range(1, L):
        w1_ref[s * blk:(s + 1) * blk, :] = jnp.where(lane >= s * blk, pltpu.roll(base8, s * blk, 1), 0.0).astype(BF16)
    pw_ref[...] = pw


def _s5_placements():
    L, GC, P, GL = S5_L, S5_GC, S5_P, S5_GL
    plt = np.zeros((GL, L * GC, L * GL * GC), np.float32)
    pls = np.zeros((GL, 2 * P, 2 * GL * P), np.float32)
    for gl in range(GL):
        for t in range(L):
            for c in range(GC):
                plt[gl, t * GC + c, t * GL * GC + gl * GC + c] = 1.0
        for ri in range(2):
            for p in range(P):
                pls[gl, ri * P + p, ri * GL * P + gl * P + p] = 1.0
    return jnp.asarray(plt, BF16), jnp.asarray(pls, BF16)


def _s5_prep(a_re, a_im, log_dt, b_re, b_im, c_re, c_im):
    g, p = a_re.shape
    L, GC, GL = S5_L, S5_GC, S5_GL
    W = L * GC
    TW = L * GL * GC
    SW = 2 * GL * p
    nblk = g // GL
    a_re2 = jnp.concatenate([a_re, a_re], axis=1)
    a_im2 = jnp.concatenate([a_im, a_im], axis=1)
    arc, aic = a_re2[:, :, None], a_im2[:, :, None]
    arr, air = a_re2[:, None, :], a_im2[:, None, :]
    ldt = log_dt[:, None, None]
    btr, bti = jnp.transpose(b_re, (0, 2, 1)), jnp.transpose(b_im, (0, 2, 1))
    x1 = jnp.tile(jnp.concatenate([btr, bti], axis=2), (1, L, 1))
    x2 = jnp.tile(jnp.concatenate([bti, btr], axis=2), (1, L, 1))
    ctr, cti = jnp.transpose(c_re, (0, 2, 1)), jnp.transpose(c_im, (0, 2, 1))
    y1 = jnp.tile(jnp.concatenate([ctr, cti], axis=1), (1, 1, L))
    y2 = jnp.tile(jnp.concatenate([cti, ctr], axis=1), (1, 1, L))
    plt, pls = _s5_placements()

    def spec(shape):
        return pl.BlockSpec((GL,) + shape, lambda i: (i, 0, 0))

    const3 = lambda i: (0, 0, 0)
    out3 = lambda i: (i, 0, 0)
    return pl.pallas_call(
        _s5_prep_kernel,
        grid=(nblk,),
        in_specs=[spec((2 * p, 1)), spec((2 * p, 1)), spec((1, 2 * p)), spec((1, 2 * p)), spec((1, 1)),
                  spec((W, 2 * p)), spec((W, 2 * p)), spec((2 * p, W)), spec((2 * p, W)),
                  pl.BlockSpec(plt.shape, const3), pl.BlockSpec(pls.shape, const3)],
        out_specs=[pl.BlockSpec((None, TW, TW), out3), pl.BlockSpec((None, TW, SW), out3),
                   pl.BlockSpec((None, SW, TW), out3), pl.BlockSpec((None, 16, SW), out3)],
        out_shape=[jax.ShapeDtypeStruct((nblk, TW, TW), BF16), jax.ShapeDtypeStruct((nblk, TW, SW), BF16),
                   jax.ShapeDtypeStruct((nblk, SW, TW), BF16), jax.ShapeDtypeStruct((nblk, 16, SW), F32)],
        compiler_params=_cparams(("parallel",)),
        name="s5_prep",
    )(arc, aic, arr, air, ldt, x1, x2, y1, y2, plt, pls)


def _s5_core_kernel(*refs, nrows, nsq, has_state):
    if has_state:
        hn_ref, w1_ref, w2_ref, w3_ref, pw_ref, s0_ref, y_ref, fin_ref = refs
    else:
        hn_ref, w1_ref, w2_ref, w3_ref, pw_ref, y_ref, fin_ref, x_scr, p_scr, s1_scr, sin_scr = refs
    L = S5_L
    half = S5_GL * S5_P
    dot = functools.partial(jnp.dot, preferred_element_type=F32)

    ns = half // 128
    strips = lambda a: [a[:, s * 128:(s + 1) * 128] for s in range(2 * ns)]
    join = lambda parts: jnp.concatenate(parts, axis=1)

    def scale_add(a, k, c):
        out = [None] * (2 * ns)
        for st in range(ns):
            wr = pw_ref[k:k + 1, st * 128:(st + 1) * 128]
            wi = pw_ref[8 + k:9 + k, half + st * 128:half + (st + 1) * 128]
            ar, ai = a[st], a[ns + st]
            out[st] = c[st] + (ar * wr - ai * wi)
            out[ns + st] = c[ns + st] + (ar * wi + ai * wr)
        return out

    u = jnp.concatenate([hn_ref[pl.ds(s, nrows, stride=L), :] for s in range(L)], axis=1).astype(BF16)
    x = dot(u, w2_ref[...])
    y_local = dot(u, w1_ref[...])
    if has_state:
        s_in = s0_ref[...]
        fin_ref[...] = join(scale_add(strips(s_in), 0, strips(x)))
    else:
        rps = nrows // nsq
        n1, n2 = rps // 2, rps // 4
        every = range(2 * ns)

        def put(ref, parts, start, n, stride):
            for s in every:
                ref[s, pl.ds(start, n, stride=stride), :] = parts[s]

        def get(ref, start, n, stride):
            return [ref[s, pl.ds(start, n, stride=stride), :] for s in every]

        row = lax.broadcasted_iota(jnp.int32, (n2, 1), 0)

        def shift(parts, sh):
            if sh % 8 == 0:
                return [jnp.concatenate([jnp.zeros((sh, 128), F32), a[:n2 - sh]], axis=0) for a in parts]
            return [jnp.where(row >= sh, pltpu.roll(a, sh, 0), 0.0) for a in parts]

        put(x_scr, strips(x), 0, nrows, 1)
        for sq in range(nsq):
            e0, o0 = get(x_scr, sq * rps, n1, 2), get(x_scr, sq * rps + 1, n1, 2)
            put(p_scr, scale_add(e0, 0, o0), sq * n1, n1, 1)
            e1, o1 = get(p_scr, sq * n1, n2, 2), get(p_scr, sq * n1 + 1, n2, 2)
            z = scale_add(e1, 1, o1)
            sh, k = 1, 2
            while sh < n2:
                z = scale_add(shift(z, sh), k, z)
                sh *= 2
                k += 1
            fin_ref[sq] = join([a[n2 - 1:n2, :] for a in z])
            s2 = shift(z, 1)
            put(s1_scr, s2, sq * n1, n2, 2)
            put(s1_scr, scale_add(s2, 1, e1), sq * n1 + 1, n2, 2)
            s1 = get(s1_scr, sq * n1, n1, 1)
            put(sin_scr, s1, sq * rps, n1, 2)
            put(sin_scr, scale_add(s1, 0, e0), sq * rps + 1, n1, 2)
        s_in = join(get(sin_scr, 0, nrows, 1))
    y = y_local + dot(s_in.astype(BF16), w3_ref[...])
    lanes = y_ref.shape[1]
    for t in range(L):
        y_ref[pl.ds(t, nrows, stride=L), :] = y[:, t * lanes:(t + 1) * lanes]


def _s5_core(hn, w1, w2, w3, pw, s0, *, bsz, seq):
    m, d = hn.shape
    L = S5_L
    nblk, tw, sw = w2.shape
    lanes = d // nblk
    has_state = s0 is not None
    assert has_state == (seq == L) and seq % L == 0 and seq // L <= 256
    if has_state:
        tile, nb = m, 1
        fin_shape = jax.ShapeDtypeStruct((bsz, nblk * sw), F32)
        fin_spec = pl.BlockSpec((bsz, sw), lambda g, b: (0, g))
    else:
        nsq = S5_SEQ_PER_STEP if bsz % S5_SEQ_PER_STEP == 0 else 1
        tile, nb = nsq * seq, bsz // nsq
        fin_shape = jax.ShapeDtypeStruct((bsz, 1, nblk * sw), F32)
        fin_spec = pl.BlockSpec((nsq, 1, sw), lambda g, b: (b, 0, g))
    nrows = tile // L
    wmap = lambda g, b: (g, 0, 0)
    in_specs = [
        pl.BlockSpec((tile, lanes), lambda g, b: (b, g)),
        pl.BlockSpec((None, tw, tw), wmap), pl.BlockSpec((None, tw, sw), wmap),
        pl.BlockSpec((None, sw, tw), wmap), pl.BlockSpec((None, 16, sw), wmap),
    ]
    args = [hn, w1, w2, w3, pw]
    if has_state:
        in_specs.append(pl.BlockSpec((bsz, sw), lambda g, b: (0, g)))
        args.append(s0)
    return pl.pallas_call(
        functools.partial(_s5_core_kernel, nrows=nrows, nsq=1 if has_state else nsq, has_state=has_state),
        grid=(nblk, nb),
        in_specs=in_specs,
        out_specs=[pl.BlockSpec((tile, lanes), lambda g, b: (b, g)), fin_spec],
        out_shape=[jax.ShapeDtypeStruct((m, d), F32), fin_shape],
        scratch_shapes=[] if has_state else [pltpu.VMEM((sw // 128, nrows // r, 128), F32) for r in (1, 2, 2, 1)],
        compiler_params=_cparams(("parallel", "parallel")),
        name="s5_core",
    )(*args)


def _gelu_tanh(x):
    return 0.5 * x * (1.0 + jnp.tanh(0.7978845608028654 * (x + 0.044715 * (x * x * x))))


def _s5_out_kernel(x_ref, ln_ref, y_ref, d_ref, w_ref, out_ref, *, d):
    x = x_ref[...]
    h = _rms(x, ln_ref[...])
    y = y_ref[...].astype(F32) + d_ref[...] * h
    z = _gelu_tanh(y).astype(BF16)
    ag = jnp.dot(z, w_ref[...], preferred_element_type=F32)
    out_ref[...] = x + ag[:, :d] * _sigmoid(ag[:, d:])


def _s5_out(x, ln, y, d_skip, w_glu, layer_j):
    m, d = x.shape
    tm = min(1024, m)
    return pl.pallas_call(
        functools.partial(_s5_out_kernel, d=d),
        grid=(m // tm,),
        in_specs=[
            pl.BlockSpec((tm, d), lambda i: (i, 0)),
            pl.BlockSpec((1, d), lambda i: (0, 0)),
            pl.BlockSpec((tm, d), lambda i: (i, 0)),
            pl.BlockSpec((1, d), lambda i: (0, 0)),
            pl.BlockSpec((None, d, 2 * d), lambda i: (layer_j, 0, 0)),
        ],
        out_specs=pl.BlockSpec((tm, d), lambda i: (i, 0)),
        out_shape=jax.ShapeDtypeStruct((m, d), F32),
        compiler_params=_cparams(("parallel",)),
        name="s5_out",
    )(x, ln, y, d_skip, w_glu)


def _unpack_state(fin, bsz):
    f = fin.reshape(bsz, -1, 2, S5_GL, S5_P)
    return f[:, :, 0].reshape(bsz, -1, S5_P), f[:, :, 1].reshape(bsz, -1, S5_P)


def _layer(i, st, wts, w_bf):
    j = i // 2
    depth = wts["ln_ffn1"].shape[0]
    x, bsz, seq = st["x"], st["bsz"], st["seq"]
    is_hgrn = i % 2 == 0

    def ffn(name, x, ln, gain2, mix=None, **kw):
        key = (name, i)
        kw = dict(dict(hn_dtype=None, final_norm=False), **kw)
        if key in w_bf:
            w_in, w_out = w_bf[key]
            out, hn, _ = _ffn_resident(x, ln, w_in, w_out, gain2, mix=mix, **kw)
        else:
            if mix is not None:
                x = _hgrn_out(x, *mix)
            out, hn, w_bf[key] = _ffn_cast(x, ln, wts[name + "_w_in"], wts[name + "_w_out"], i, gain2, **kw)
        return out, hn

    mix = None

    x, hn = ffn("ffn1", x, wts["ln_ffn1"][i:i + 1], wts["ln_mix"][i:i + 1], hn_dtype=BF16 if is_hgrn else F32)
    if is_hgrn:
        cum, masks = _level_consts(min(seq, ROW_BLOCK))
        q, k, v, b, sg = _hgrn_gates(hn, wts["hgrn_w_in"], j, wts["hgrn_lb_logits"], i, cum)
        og, st["new_hgrn"] = _hgrn_scan(q, k, v, b, sg, wts["hgrn_gnorm"][j:j + 1], masks, st["hgrn"], j,
                                        st["new_hgrn"], bsz=bsz, seq=seq)
        mix = (og, wts["hgrn_w_out"], j)
    else:
        w1, w2, w3, pw = wts["s5_mats"][j]
        s0 = None
        if st["re"] is not None:
            pack = lambda a: a[j].reshape(bsz, -1, 1, S5_GL, S5_P)
            s0 = jnp.concatenate([pack(st["re"]), pack(st["im"])], axis=2).reshape(bsz, -1)
        y, fin = _s5_core(hn, w1, w2, w3, pw, s0, bsz=bsz, seq=seq)
        f_re, f_im = _unpack_state(fin, bsz)
        st["new_re"].append(f_re)
        st["new_im"].append(f_im)
        x = _s5_out(x, wts["ln_mix"][i:i + 1], y, wts["s5_d"][j:j + 1], wts["s5_w_glu"], j)
    last = i == depth - 1
    gain2 = wts["ln_final"] if last else wts["ln_ffn2"][i:i + 1]
    x, _ = ffn("ffn2", x, wts["ln_ffn2"][i:i + 1], gain2, mix=mix, final_norm=last)
    st["x"] = x


def kernel(x_prompt, x_sample, state_hgrn, state_s5_re, state_s5_im, ln_ffn1, ffn1_w_in, ffn1_w_out, ln_mix, ln_ffn2, ffn2_w_in, ffn2_w_out, hgrn_lb_logits, hgrn_w_in, hgrn_gnorm, hgrn_w_out, s5_a_re, s5_a_im, s5_log_dt, s5_b_re, s5_b_im, s5_c_re, s5_c_im, s5_d, s5_w_glu, ln_final):
    depth, d = ln_ffn1.shape
    wts = dict(
        ln_ffn1=ln_ffn1, ln_mix=ln_mix, ln_ffn2=ln_ffn2, ln_final=ln_final.reshape(1, -1),
        ffn1_w_in=ffn1_w_in, ffn1_w_out=ffn1_w_out, ffn2_w_in=ffn2_w_in, ffn2_w_out=ffn2_w_out,
        hgrn_lb_logits=hgrn_lb_logits, hgrn_w_in=hgrn_w_in.astype(BF16), hgrn_gnorm=hgrn_gnorm,
        hgrn_w_out=hgrn_w_out.astype(BF16), s5_d=s5_d, s5_w_glu=s5_w_glu.astype(BF16),
        s5_mats=[_s5_prep(s5_a_re[j], s5_a_im[j], s5_log_dt[j], s5_b_re[j], s5_b_im[j], s5_c_re[j], s5_c_im[j])
                 for j in range(s5_a_re.shape[0])],
    )

    def stream(x3, hgrn, re, im):
        bsz, seq, _ = x3.shape
        return dict(x=x3.reshape(bsz * seq, d), bsz=bsz, seq=seq, hgrn=hgrn, re=re, im=im,
                    new_hgrn=None, new_re=[], new_im=[])

    sample = stream(x_sample, state_hgrn, state_s5_re, state_s5_im)
    prompt = stream(x_prompt, None, None, None)
    w_bf = {}
    for i in range(depth):
        _layer(i, sample, wts, w_bf)
        _layer(i, prompt, wts, w_bf)

    def outs(st, x3):
        return (st["x"].reshape(x3.shape), st["new_hgrn"], jnp.stack(st["new_re"]), jnp.stack(st["new_im"]))

    y_p, hg_p, re_p, im_p = outs(prompt, x_prompt)
    y_s, hg_s, re_s, im_s = outs(sample, x_sample)
    return (y_p, y_s, hg_p, re_p, im_p, hg_s, re_s, im_s)
```
